```python
import math
import jax, jax.numpy as jnp
from jax import lax
import numpy as np

D_MODEL = 1024
BATCH = 8
SEQ = 2048
DEPTH = 2
DEC_BATCH = 128
DEC_SEQ = 8
PAST_LEN = 8192
PAGE_SIZE = 128

N_EVEN = (DEPTH + 1) // 2
N_ODD = DEPTH // 2
ROPE_THETA = 500000.0
EPS = 1e-6
Q_BLOCK = 128
A_HEADS = 4
A_DH = 64
A_DV = 2 * A_DH
A_ROT = A_DH // 4
B_HEADS = 4
Q_LORA = 384
KV_LORA = 256
QK_NOPE = 128
QK_ROPE = 64
B_DV = 128
CKV_DIM = KV_LORA + QK_ROPE
EVEN_SIZES = (2 * A_HEADS * A_DH, 2 * A_HEADS * A_DH, A_HEADS * A_DV, Q_LORA, KV_LORA, QK_ROPE)
IN_EVEN = 2 * A_HEADS * A_DH * 2 + A_HEADS * A_DV + Q_LORA + KV_LORA + QK_ROPE
D_MIX_EVEN = A_HEADS * A_DV + B_HEADS * B_DV
C_HEADS = 16
C_KV_HEADS = 8
C_DH = 64
ODD_SIZES = (C_HEADS * C_DH, C_KV_HEADS * C_DH, C_KV_HEADS * C_DH, C_HEADS)
IN_ODD = C_HEADS * C_DH + 2 * C_KV_HEADS * C_DH + C_HEADS
D_FF = 2816
CONV_W = 3

kernel_name = "hybrid_diff_mla_fox_convffn_step"


def lambda_init(layer):
    return 0.8 - 0.6 * math.exp(-0.3 * layer)


def rmsnorm(x, g):
    x32 = x.astype(jnp.float32)
    y = x32 * lax.rsqrt(jnp.mean(x32 * x32, axis=-1, keepdims=True) + EPS)
    return (y * g.astype(jnp.float32)).astype(x.dtype)


def rope(x, pos):
    half = x.shape[-1] // 2
    inv = jnp.power(jnp.float32(ROPE_THETA), -jnp.arange(half, dtype=jnp.float32) / half)
    ang = pos.astype(jnp.float32)[:, None] * inv[None, :]
    shape = (1, x.shape[1]) + (1,) * (x.ndim - 3) + (half,)
    cos = jnp.cos(ang).reshape(shape)
    sin = jnp.sin(ang).reshape(shape)
    x32 = x.astype(jnp.float32)
    x1, x2 = x32[..., :half], x32[..., half:]
    return jnp.concatenate([x1 * cos - x2 * sin, x2 * cos + x1 * sin], axis=-1).astype(x.dtype)


def partial_rope(x, pos):
    return jnp.concatenate([rope(x[..., :A_ROT], pos), x[..., A_ROT:]], axis=-1)


def _split(x, sizes):
    out, start = [], 0
    for s in sizes:
        out.append(x[..., start:start + s])
        start += s
    return out


def gather_pages(cache, layer, page_table):
    g = cache[layer, page_table]
    return g.reshape((page_table.shape[0], -1) + cache.shape[3:])


def _attend_block(q, q_pos, segs, scale, q_bias):
    bsz, tq, h, dk = q.shape
    g = segs[0][0].shape[2]
    r = h // g
    qg = q.reshape(bsz, tq, g, r, dk)
    if q_bias is not None:
        qb = q_bias.astype(jnp.float32).reshape(bsz, tq, g, r).transpose(0, 2, 3, 1)[..., None]
    logits = []
    for k, v, k_pos, k_bias in segs:
        s = jnp.einsum('bqgrd,bkgd->bgrqk', qg, k).astype(jnp.float32) * scale
        if q_bias is not None:
            kb = k_bias.astype(jnp.float32).reshape(bsz, k.shape[1], g, r).transpose(0, 2, 3, 1)[..., None, :]
            s = s + (qb - kb)
        s = jnp.where(k_pos[None, :] <= q_pos[:, None], s, -jnp.inf)
        logits.append(s)
    p = jax.nn.softmax(jnp.concatenate(logits, axis=-1), axis=-1)
    out, start = None, 0
    for k, v, _, _ in segs:
        tk = k.shape[1]
        o = jnp.einsum('bgrqk,bkgd->bqgrd', p[..., start:start + tk].astype(v.dtype), v)
        out = o if out is None else out + o
        start += tk
    return out.reshape(bsz, tq, h, out.shape[-1])


def attention(q, q_pos, segs, scale, q_bias=None):
    bsz, tq, h, dk = q.shape
    if tq <= Q_BLOCK or tq % Q_BLOCK:
        return _attend_block(q, q_pos, segs, scale, q_bias)
    nb = tq // Q_BLOCK
    qb = q.reshape(bsz, nb, Q_BLOCK, h, dk).transpose(1, 0, 2, 3, 4)
    pb = q_pos.reshape(nb, Q_BLOCK)
    if q_bias is None:
        out = lax.map(lambda a: _attend_block(a[0], a[1], segs, scale, None), (qb, pb))
    else:
        bb = q_bias.reshape(bsz, nb, Q_BLOCK, h).transpose(1, 0, 2, 3)
        out = lax.map(lambda a: _attend_block(a[0], a[1], segs, scale, a[2]), (qb, pb, bb))
    return out.transpose(1, 0, 2, 3, 4).reshape(bsz, tq, h, out.shape[-1])


def even_mixer(h, pos, past, layer, w_in, w_q_up, w_kv_uk, w_kv_uv, g_q_lat, g_kv_lat, lam, g_sub, w_out):
    bsz, t, _ = h.shape
    qa, ka, va, cq, ckv_lat, kr = _split(h @ w_in, EVEN_SIZES)
    qa = partial_rope(qa.reshape(bsz, t, 2, A_HEADS, A_DH), pos)
    ka = partial_rope(ka.reshape(bsz, t, 2, A_HEADS, A_DH), pos)
    va = va.reshape(bsz, t, A_HEADS, A_DV)
    q = jnp.einsum('btr,rhe->bthe', rmsnorm(cq, g_q_lat), w_q_up)
    q_abs = jnp.concatenate([jnp.einsum('bthn,lhn->bthl', q[..., :QK_NOPE], w_kv_uk),
                             rope(q[..., QK_NOPE:], pos)], axis=-1)
    ckv = jnp.concatenate([rmsnorm(ckv_lat, g_kv_lat), rope(kr[:, :, None, :], pos)[:, :, 0]], axis=-1)
    sets = ([] if past is None else [past]) + [(ka, va, ckv, pos)]
    seg1 = [(k[:, :, 0], v, p, None) for k, v, _, p in sets]
    seg2 = [(k[:, :, 1], v, p, None) for k, v, _, p in sets]
    segb = [(c[:, :, None, :], c[:, :, None, :KV_LORA], p, None) for _, _, c, p in sets]
    lam_init = lambda_init(layer)
    lam32 = lam.astype(jnp.float32)
    lam_val = jnp.exp(jnp.sum(lam32[0] * lam32[1])) - jnp.exp(jnp.sum(lam32[2] * lam32[3])) + lam_init
    o1 = attention(qa[:, :, 0], pos, seg1, A_DH ** -0.5)
    o2 = attention(qa[:, :, 1], pos, seg2, A_DH ** -0.5)
    oa = rmsnorm(o1 - lam_val.astype(o1.dtype) * o2, g_sub) * (1.0 - lam_init)
    o_lat = attention(q_abs, pos, segb, (QK_NOPE + QK_ROPE) ** -0.5)
    ob = jnp.einsum('bthl,lhv->bthv', o_lat, w_kv_uv)
    y = jnp.concatenate([oa.reshape(bsz, t, -1), ob.reshape(bsz, t, -1)], axis=-1) @ w_out
    return y, (ka, va, ckv)


def odd_mixer(h, pos, past, w_in, b_forget, w_out):
    bsz, t, _ = h.shape
    q, k, v, f = _split(h @ w_in, ODD_SIZES)
    q = q.reshape(bsz, t, C_HEADS, C_DH)
    k = k.reshape(bsz, t, C_KV_HEADS, C_DH)
    v = v.reshape(bsz, t, C_KV_HEADS, C_DH)
    logf = jax.nn.log_sigmoid((f + b_forget).astype(jnp.float32))
    if past is None:
        c_new = jnp.cumsum(logf, axis=1)
        segs = [(k, v, pos, c_new)]
    else:
        pk, pv, plogf, ppos = past
        c_past = jnp.cumsum(plogf.astype(jnp.float32), axis=1)
        c_new = c_past[:, -1:] + jnp.cumsum(logf, axis=1)
        segs = [(pk, pv, ppos, c_past), (k, v, pos, c_new)]
    o = attention(q, pos, segs, C_DH ** -0.5, q_bias=c_new)
    return o.reshape(bsz, t, -1) @ w_out, (k, v, logf.astype(h.dtype))


def conv_ffn(h, buf, w_gate, w_up, conv_w, conv_b, w_down):
    t = h.shape[1]
    gp = jnp.concatenate([buf, h @ w_gate], axis=1)
    gc = conv_b + sum(conv_w[i] * gp[:, i:i + t] for i in range(CONV_W))
    y = (jax.nn.silu(gc) * (h @ w_up)) @ w_down
    return y, gp[:, t:]


def setup_inputs(seed: int = 0) -> dict:
    key = jax.random.key(seed)
    ks = iter(jax.random.split(key, 40))

    def nrm(shape, scale):
        return scale * jax.random.normal(next(ks), shape, jnp.float32)

    n_pages = PAST_LEN // PAGE_SIZE
    n_pool = (DEC_BATCH * n_pages * 5) // 4
    x_prompt = nrm((BATCH, SEQ, D_MODEL), 1.0)
    x_sample = nrm((DEC_BATCH, DEC_SEQ, D_MODEL), 1.0)
    cache_a_k = nrm((N_EVEN, n_pool, PAGE_SIZE, 2, A_HEADS, A_DH), 1.0)
    cache_a_v = nrm((N_EVEN, n_pool, PAGE_SIZE, A_HEADS, A_DV), 1.0)
    cache_b_ckv = nrm((N_EVEN, n_pool, PAGE_SIZE, CKV_DIM), 1.0)
    cache_c_k = nrm((N_ODD, n_pool, PAGE_SIZE, C_KV_HEADS, C_DH), 1.0)
    cache_c_v = nrm((N_ODD, n_pool, PAGE_SIZE, C_KV_HEADS, C_DH), 1.0)
    cache_c_logf = jax.nn.log_sigmoid(2.0 + nrm((N_ODD, n_pool, PAGE_SIZE, C_HEADS), 0.5))
    state_conv = nrm((DEPTH, DEC_BATCH, CONV_W - 1, D_FF), 1.0)
    page_table = jax.random.permutation(next(ks), n_pool)[:DEC_BATCH * n_pages].reshape(DEC_BATCH, n_pages).astype(jnp.int32)
    return {
        "x_prompt": x_prompt,
        "x_sample": x_sample,
        "cache_a_k": cache_a_k,
        "cache_a_v": cache_a_v,
        "cache_b_ckv": cache_b_ckv,
        "cache_c_k": cache_c_k,
        "cache_c_v": cache_c_v,
        "cache_c_logf": cache_c_logf,
        "state_conv": state_conv,
        "page_table": page_table,
        "norm_gains": 1.0 + nrm((DEPTH, 4, D_MODEL), 0.05),
        "w_in_even": nrm((N_EVEN, D_MODEL, IN_EVEN), D_MODEL ** -0.5),
        "w_q_up": nrm((N_EVEN, Q_LORA, B_HEADS, QK_NOPE + QK_ROPE), Q_LORA ** -0.5),
        "w_kv_uk": nrm((N_EVEN, KV_LORA, B_HEADS, QK_NOPE), KV_LORA ** -0.5),
        "w_kv_uv": nrm((N_EVEN, KV_LORA, B_HEADS, B_DV), KV_LORA ** -0.5),
        "g_q_lat": 1.0 + nrm((N_EVEN, Q_LORA), 0.05),
        "g_kv_lat": 1.0 + nrm((N_EVEN, KV_LORA), 0.05),
        "diff_lambda": nrm((N_EVEN, 4, A_DH), 0.1),
        "g_diff_subln": 1.0 + nrm((N_EVEN, A_DV), 0.05),
        "w_out_even": nrm((N_EVEN, D_MIX_EVEN, D_MODEL), D_MIX_EVEN ** -0.5),
        "w_in_odd": nrm((N_ODD, D_MODEL, IN_ODD), D_MODEL ** -0.5),
        "b_forget": 2.0 + nrm((N_ODD, C_HEADS), 0.1),
        "w_out_odd": nrm((N_ODD, C_HEADS * C_DH, D_MODEL), (C_HEADS * C_DH) ** -0.5),
        "ffn_w_gate": nrm((DEPTH, D_MODEL, D_FF), D_MODEL ** -0.5),
        "ffn_w_up": nrm((DEPTH, D_MODEL, D_FF), D_MODEL ** -0.5),
        "ffn_conv_w": nrm((DEPTH, CONV_W, D_FF), CONV_W ** -0.5),
        "ffn_conv_b": nrm((DEPTH, D_FF), 0.02),
        "ffn_w_down": nrm((DEPTH, D_FF, D_MODEL), D_FF ** -0.5),
    }


def reference(x_prompt, x_sample, cache_a_k, cache_a_v, cache_b_ckv, cache_c_k, cache_c_v, cache_c_logf,
              state_conv, page_table, norm_gains, w_in_even, w_q_up, w_kv_uk, w_kv_uv, g_q_lat, g_kv_lat,
              diff_lambda, g_diff_subln, w_out_even, w_in_odd, b_forget, w_out_odd,
              ffn_w_gate, ffn_w_up, ffn_conv_w, ffn_conv_b, ffn_w_down):
    past_len = page_table.shape[1] * PAGE_SIZE

    def run(x, pos, paged):
        bsz = x.shape[0]
        past_pos = jnp.arange(past_len, dtype=jnp.int32)
        h = x
        ak, av, bckv, ck, cv, cf, conv = [], [], [], [], [], [], []
        for li in range(DEPTH):
            gn = norm_gains[li]
            hn = rmsnorm(h, gn[0])
            if li % 2 == 0:
                e = li // 2
                past = None
                if paged:
                    past = (gather_pages(cache_a_k, e, page_table), gather_pages(cache_a_v, e, page_table),
                            gather_pages(cache_b_ckv, e, page_table), past_pos)
                y, (k_rows, v_rows, c_rows) = even_mixer(
                    hn, pos, past, li, w_in_even[e], w_q_up[e], w_kv_uk[e], w_kv_uv[e], g_q_lat[e],
                    g_kv_lat[e], diff_lambda[e], g_diff_subln[e], w_out_even[e])
                ak.append(k_rows)
                av.append(v_rows)
                bckv.append(c_rows)
            else:
                o = li // 2
                past = None
                if paged:
                    past = (gather_pages(cache_c_k, o, page_table), gather_pages(cache_c_v, o, page_table),
                            gather_pages(cache_c_logf, o, page_table), past_pos)
                y, (k_rows, v_rows, f_rows) = odd_mixer(hn, pos, past, w_in_odd[o], b_forget[o], w_out_odd[o])
                ck.append(k_rows)
                cv.append(v_rows)
                cf.append(f_rows)
            h = h + rmsnorm(y, gn[1])
            buf = state_conv[li] if paged else jnp.zeros((bsz, CONV_W - 1, D_FF), x.dtype)
            f_out, new_buf = conv_ffn(rmsnorm(h, gn[2]), buf, ffn_w_gate[li], ffn_w_up[li], ffn_conv_w[li],
                                      ffn_conv_b[li], ffn_w_down[li])
            h = h + rmsnorm(f_out, gn[3])
            conv.append(new_buf)
        return h, (jnp.stack(ak), jnp.stack(av), jnp.stack(bckv), jnp.stack(ck), jnp.stack(cv),
                   jnp.stack(cf), jnp.stack(conv))

    y_prompt, rp = run(x_prompt, jnp.arange(x_prompt.shape[1], dtype=jnp.int32), False)
    y_sample, rs = run(x_sample, past_len + jnp.arange(x_sample.shape[1], dtype=jnp.int32), True)
    return (y_prompt, y_sample, rp[0], rs[0], rp[1], rs[1], rp[2], rs[2], rp[3], rs[3], rp[4], rs[4],
            rp[5], rs[5], rp[6], rs[6])
```

```python
import functools
import math

import jax
import jax.numpy as jnp
from jax import lax
from jax.experimental import pallas as pl
from jax.experimental.pallas import tpu as pltpu

F32 = jnp.float32
BF16 = jnp.bfloat16

EPS = 1e-6
ROPE_THETA = 500000.0
A_HEADS, A_DH, A_DV, A_ROT = 4, 64, 128, 16
B_HEADS, Q_LORA, KV_LORA, QK_NOPE, QK_ROPE, B_DV = 4, 384, 256, 128, 64, 128
C_HEADS, C_KV_HEADS, C_DH = 16, 8, 64
CONV_W = 3
PAGE = 128
LANES = 128
NEG = -1e30
VMEM_LIMIT_BYTES = 56 * 1024 * 1024

_NT = (((1,), (1,)), ((), ()))


def _cp(sem):
    return pltpu.CompilerParams(dimension_semantics=sem, vmem_limit_bytes=VMEM_LIMIT_BYTES)


def _rms(x, g):
    return x * lax.rsqrt(jnp.mean(x * x, axis=-1, keepdims=True) + EPS) * g


def _dot(a, b):
    return jnp.dot(a, b, preferred_element_type=F32)


def _dot_nt(a, b):
    return lax.dot_general(a, b, _NT, preferred_element_type=F32)


def _rope_chunk(x, c, sl, sr, shift):
    return x * c + pltpu.roll(x, LANES - shift, 1) * sl + pltpu.roll(x, shift, 1) * sr


def _rope_tables(pos, rot):
    half = rot // 2
    inv = jnp.power(jnp.float32(ROPE_THETA), -jnp.arange(half, dtype=F32) / half)
    ang = pos.astype(F32)[:, None] * inv[None, :]
    cos, sin = jnp.cos(ang), jnp.sin(ang)
    t = pos.shape[0]
    one = jnp.ones((t, 64 - rot), F32)
    z_rest = jnp.zeros((t, 64 - rot), F32)
    z_half = jnp.zeros((t, half), F32)
    c = jnp.concatenate([cos, cos, one], axis=1)
    sl = jnp.concatenate([-sin, z_half, z_rest], axis=1)
    sr = jnp.concatenate([z_half, sin, z_rest], axis=1)
    return tuple(jnp.tile(a, (1, 2)) for a in (c, sl, sr))


def _even_in_kernel(h_ref, g0_ref, win_ref, gq_ref, gkv_ref, wq_ref, wuk_ref,
                    ca_ref, sla_ref, sra_ref, cb_ref, slb_ref, srb_ref,
                    qa_ref, ka_ref, kab_ref, va_ref, vab_ref, qlat_ref, qrp_ref, ckv_ref, ckvb_ref):
    hn = _rms(h_ref[0], g0_ref[...]).astype(BF16)
    z = _dot(hn, win_ref[...])
    ca, sla, sra = ca_ref[...], sla_ref[...], sra_ref[...]
    cb, slb, srb = cb_ref[...], slb_ref[...], srb_ref[...]
    for j in range(4):
        sl = slice(LANES * j, LANES * (j + 1))
        qa_ref[0, :, sl] = (_rope_chunk(z[:, sl], ca, sla, sra, A_ROT // 2) * (A_DH ** -0.5)).astype(BF16)
        kr = _rope_chunk(z[:, 512 + LANES * j:512 + LANES * (j + 1)], ca, sla, sra, A_ROT // 2)
        ka_ref[0, :, sl] = kr
        kab_ref[0, :, sl] = kr.astype(BF16)
    va = z[:, 1024:1536]
    va_ref[0] = va
    vab_ref[0] = va.astype(BF16)
    cqn = _rms(z[:, 1536:1920], gq_ref[...]).astype(BF16)
    q2 = _dot(cqn, wq_ref[...])
    for h in range(B_HEADS):
        qn = q2[:, LANES * h:LANES * (h + 1)].astype(BF16)
        qlat_ref[0, :, KV_LORA * h:KV_LORA * (h + 1)] = _dot(qn, wuk_ref[h]).astype(BF16)
        qr = _rope_chunk(q2[:, 512 + LANES * h:512 + LANES * (h + 1)], cb, slb, srb, QK_ROPE // 2)
        qrp_ref[0, :, LANES * h:LANES * (h + 1)] = qr.astype(BF16)
    cn = _rms(z[:, 1920:2176], gkv_ref[...])
    krr = _rope_chunk(z[:, 2176:2304], cb, slb, srb, QK_ROPE // 2)
    ckv_ref[0, :, 0:KV_LORA] = cn
    ckv_ref[0, :, KV_LORA:KV_LORA + QK_ROPE] = krr[:, :QK_ROPE]
    ckvb_ref[0, :, 0:KV_LORA] = cn.astype(BF16)
    ckvb_ref[0, :, KV_LORA:KV_LORA + LANES] = krr.astype(BF16)


def _tab_spec(tm, n_tab_tiles):
    return pl.BlockSpec((tm, LANES), lambda b, i: (i % n_tab_tiles, 0))


def _full_spec(shape):
    nd = len(shape)
    return pl.BlockSpec(shape, lambda b, i: (0,) * nd)


def _even_in(h, g0, win, gq, gkv, wq, wuk, tabs_a, tabs_b, tm):
    bk, tk, d = h.shape
    n_tab = tabs_a[0].shape[0] // tm
    tok = lambda w: pl.BlockSpec((1, tm, w), lambda b, i: (b, i, 0))
    out_shapes = [
        jax.ShapeDtypeStruct((bk, tk, 512), BF16),
        jax.ShapeDtypeStruct((bk, tk, 512), F32),
        jax.ShapeDtypeStruct((bk, tk, 512), BF16),
        jax.ShapeDtypeStruct((bk, tk, 512), F32),
        jax.ShapeDtypeStruct((bk, tk, 512), BF16),
        jax.ShapeDtypeStruct((bk, tk, 1024), BF16),
        jax.ShapeDtypeStruct((bk, tk, 512), BF16),
        jax.ShapeDtypeStruct((bk, tk, 320), F32),
        jax.ShapeDtypeStruct((bk, tk, 384), BF16),
    ]
    return pl.pallas_call(
        _even_in_kernel,
        grid=(bk, tk // tm),
        in_specs=[tok(d), _full_spec(g0.shape), _full_spec(win.shape), _full_spec(gq.shape),
                  _full_spec(gkv.shape), _full_spec(wq.shape), _full_spec(wuk.shape)]
                 + [_tab_spec(tm, n_tab)] * 6,
        out_specs=[tok(s.shape[-1]) for s in out_shapes],
        out_shape=out_shapes,
        compiler_params=_cp(("parallel", "parallel")),
        name="even_in_proj",
    )(h, g0, win, gq, gkv, wq, wuk, *tabs_a, *tabs_b)


def _split3(x):
    hi = x.astype(BF16)
    r = x - hi.astype(F32)
    mid = r.astype(BF16)
    lo = (r - mid.astype(F32)).astype(BF16)
    return hi, mid, lo


def _tri_dot(x, tri):
    n = x.shape[0]
    hi, mid, lo = _split3(x)
    y = _dot(jnp.concatenate([hi, mid, lo], axis=0), tri)
    return y[0:n] + y[n:2 * n] + y[2 * n:3 * n]


def _log_sigmoid(x):
    return -(jnp.maximum(-x, 0.0) + jnp.log1p(jnp.exp(-jnp.abs(x))))


def _odd_in_kernel(h_ref, g0_ref, win_ref, wft_ref, bf_ref, tri_ref,
                   q_ref, k_ref, kb_ref, v_ref, vb_ref, lft_ref, ct_ref, cn_ref, carry_ref):
    i = pl.program_id(1)
    hn = _rms(h_ref[0], g0_ref[...]).astype(BF16)
    z = _dot(hn, win_ref[...])
    q_ref[0] = (z[:, 0:1024] * (C_DH ** -0.5)).astype(BF16)
    k = z[:, 1024:1536]
    k_ref[0] = k
    kb_ref[0] = k.astype(BF16)
    v = z[:, 1536:2048]
    v_ref[0] = v
    vb_ref[0] = v.astype(BF16)
    ft = _dot_nt(wft_ref[...], hn)
    lft = _log_sigmoid(ft + bf_ref[...])
    lft_ref[0] = lft

    @pl.when(i == 0)
    def _():
        carry_ref[...] = jnp.zeros_like(carry_ref)

    c = _tri_dot(lft, tri_ref[...]) + carry_ref[...]
    carry_ref[...] = c[:, -1:]
    ct_ref[0] = c
    c128 = jnp.concatenate([c, jnp.zeros((LANES - C_HEADS, c.shape[1]), F32)], axis=0)
    cn_ref[0] = jnp.transpose(c128)[:, 0:C_HEADS]


def _odd_in(h, g0, win, wft, bf, tm):
    bk, tk, d = h.shape
    tri = jnp.triu(jnp.ones((tm, tm), F32)).astype(BF16)
    tok = lambda w: pl.BlockSpec((1, tm, w), lambda b, i: (b, i, 0))
    tokt = pl.BlockSpec((1, C_HEADS, tm), lambda b, i: (b, 0, i))
    out_shapes = [
        jax.ShapeDtypeStruct((bk, tk, 1024), BF16),
        jax.ShapeDtypeStruct((bk, tk, 512), F32),
        jax.ShapeDtypeStruct((bk, tk, 512), BF16),
        jax.ShapeDtypeStruct((bk, tk, 512), F32),
        jax.ShapeDtypeStruct((bk, tk, 512), BF16),
        jax.ShapeDtypeStruct((bk, C_HEADS, tk), F32),
        jax.ShapeDtypeStruct((bk, C_HEADS, tk), F32),
        jax.ShapeDtypeStruct((bk, tk, C_HEADS), F32),
    ]
    return pl.pallas_call(
        _odd_in_kernel,
        grid=(bk, tk // tm),
        in_specs=[tok(d), _full_spec(g0.shape), _full_spec(win.shape), _full_spec(wft.shape),
                  _full_spec(bf.shape), _full_spec(tri.shape)],
        out_specs=[tok(1024), tok(512), tok(512), tok(512), tok(512), tokt, tokt, tok(C_HEADS)],
        out_shape=out_shapes,
        scratch_shapes=[pltpu.VMEM((C_HEADS, 1), F32)],
        compiler_params=_cp(("parallel", "arbitrary")),
        name="odd_in_proj",
    )(h, g0, win, wft, bf, tri)


def _even_out_kernel(h_ref, oa_ref, olat_ref, wuv_ref, wout_ref, g1_ref, out_ref):
    parts = [oa_ref[0]]
    for hd in range(B_HEADS):
        parts.append(_dot(olat_ref[0, :, KV_LORA * hd:KV_LORA * (hd + 1)], wuv_ref[hd]).astype(BF16))
    y = _dot(jnp.concatenate(parts, axis=1), wout_ref[...])
    out_ref[0] = h_ref[0] + _rms(y, g1_ref[...])


def _odd_out_kernel(h_ref, o_ref, wout_ref, g1_ref, out_ref):
    y = _dot(o_ref[0], wout_ref[...])
    out_ref[0] = h_ref[0] + _rms(y, g1_ref[...])


def _mix_out(kern, h, acts, consts, tm, name):
    bk, tk, d = h.shape
    tok = lambda w: pl.BlockSpec((1, tm, w), lambda b, i: (b, i, 0))
    return pl.pallas_call(
        kern,
        grid=(bk, tk // tm),
        in_specs=[tok(d)] + [tok(a.shape[-1]) for a in acts] + [_full_spec(c.shape) for c in consts],
        out_specs=tok(d),
        out_shape=jax.ShapeDtypeStruct(h.shape, F32),
        compiler_params=_cp(("parallel", "parallel")),
        name=name,
    )(h, *acts, *consts)


def _ffn_kernel(short_seq, *refs):
    if short_seq:
        (h_ref, g2_ref, wg_ref, wu_ref, cw_ref, cb_ref, wd_ref, g3_ref, pa_ref, pb_ref,
         out_ref, gate_ref, hn_ref, acc_ref) = refs
    else:
        (h_ref, g2_ref, wg_ref, wu_ref, cw_ref, cb_ref, wd_ref, g3_ref,
         out_ref, tail_ref, hn_ref, acc_ref, carry_ref) = refs
    i = pl.program_id(1)
    c = pl.program_id(2)

    @pl.when(c == 0)
    def _():
        hn_ref[...] = _rms(h_ref[0], g2_ref[...]).astype(BF16)
        acc_ref[...] = jnp.zeros_like(acc_ref)

    hn = hn_ref[...]
    g = _dot(hn, wg_ref[...])
    u = _dot(hn, wu_ref[...])
    tm = g.shape[0]
    row = lax.broadcasted_iota(jnp.int32, g.shape, 0)
    r1 = pltpu.roll(g, 1, 0)
    r2 = pltpu.roll(g, 2, 0)
    if short_seq:
        t = row % 8
        gm1 = jnp.where(t == 0, pb_ref[0], r1)
        gm2 = jnp.where(t < 2, pa_ref[0], r2)
        gate_ref[0] = g
    else:
        top = jnp.where(i == 0, 0.0, carry_ref[c])
        gm1 = jnp.where(row == 0, top[7:8], r1)
        gm2 = jnp.where(row == 0, top[6:7], jnp.where(row == 1, top[7:8], r2))
        carry_ref[c] = g[tm - 8:tm]
        tail_ref[0] = g[tm - 8:tm]
    cw = cw_ref[...]
    gc = cb_ref[...] + ((cw[0:1] * gm2 + cw[1:2] * gm1) + cw[2:3] * g)
    act = (jax.nn.silu(gc) * u).astype(BF16)
    acc_ref[...] += _dot(act, wd_ref[...])

    @pl.when(c == pl.num_programs(2) - 1)
    def _():
        out_ref[0] = h_ref[0] + _rms(acc_ref[...], g3_ref[...])


def _ffn(h, g2, wg, wu, cw, cb, wd, g3, prev, tm, tf):
    bk, tk, d = h.shape
    dff = wg.shape[1]
    nc = dff // tf
    short_seq = prev is not None
    tok = pl.BlockSpec((1, tm, d), lambda b, i, c: (b, i, 0))
    const = lambda shape: pl.BlockSpec(shape, lambda b, i, c: (0,) * len(shape))
    in_specs = [tok, const(g2.shape),
                pl.BlockSpec((d, tf), lambda b, i, c: (0, c)),
                pl.BlockSpec((d, tf), lambda b, i, c: (0, c)),
                pl.BlockSpec((CONV_W, tf), lambda b, i, c: (0, c)),
                pl.BlockSpec((1, tf), lambda b, i, c: (0, c)),
                pl.BlockSpec((tf, d), lambda b, i, c: (c, 0)),
                const(g3.shape)]
    args = [h, g2, wg, wu, cw, cb, wd, g3]
    scratch = [pltpu.VMEM((tm, d), BF16), pltpu.VMEM((tm, d), F32)]
    if short_seq:
        in_specs += [pl.BlockSpec((1, tm, tf), lambda b, i, c: (b, i, c))] * 2
        args += list(prev)
        out_specs = [tok, pl.BlockSpec((1, tm, tf), lambda b, i, c: (b, i, c))]
        out_shape = [jax.ShapeDtypeStruct(h.shape, F32), jax.ShapeDtypeStruct((bk, tk, dff), F32)]
    else:
        out_specs = [tok, pl.BlockSpec((1, 8, tf), lambda b, i, c: (b, 0, c))]
        out_shape = [jax.ShapeDtypeStruct(h.shape, F32), jax.ShapeDtypeStruct((bk, 8, dff), F32)]
        scratch.append(pltpu.VMEM((nc, 8, tf), F32))
    return pl.pallas_call(
        functools.partial(_ffn_kernel, short_seq),
        grid=(bk, tk // tm, nc),
        in_specs=in_specs,
        out_specs=out_specs,
        out_shape=out_shape,
        scratch_shapes=scratch,
        compiler_params=_cp(("parallel", "arbitrary", "arbitrary")),
        name="conv_ffn",
    )(*args)


def _flash_rows(i, tq, qk_fn, v_fn, bias_fn, scale, m_ref, l_ref, acc_ref):
    m_ref[...] = jnp.full_like(m_ref, NEG)
    l_ref[...] = jnp.zeros_like(l_ref)
    acc_ref[...] = jnp.zeros_like(acc_ref)

    def step(j, masked):
        kv = pl.ds(pl.multiple_of(j * tq, tq), tq)
        s = qk_fn(kv)
        if scale is not None:
            s = s * scale
        if bias_fn is not None:
            s = s + bias_fn(kv)
        if masked:
            row = lax.broadcasted_iota(jnp.int32, s.shape, 0) % tq
            col = lax.broadcasted_iota(jnp.int32, s.shape, 1)
            s = jnp.where(col <= row, s, NEG)
        m_prev = m_ref[...]
        m_new = jnp.maximum(m_prev, jnp.max(s, axis=-1, keepdims=True))
        alpha = jnp.exp(m_prev - m_new)
        p = jnp.exp(s - m_new)
        l_ref[...] = alpha * l_ref[...] + jnp.sum(p, axis=-1, keepdims=True)
        acc_ref[...] = alpha * acc_ref[...] + _dot(p.astype(BF16), v_fn(kv))
        m_ref[...] = m_new

    def body(j, carry):
        step(j, False)
        return carry

    lax.fori_loop(0, i, body, 0)
    step(i, True)
    return acc_ref[...] / l_ref[...]


def _half_mask(x, half):
    lane = lax.broadcasted_iota(jnp.int32, x.shape, 1)
    keep = (lane < 64) if half == 0 else (lane >= 64)
    return jnp.where(keep, x, jnp.zeros_like(x))


def _lambda_value(lam_ref, lam_init):
    lam = lam_ref[...]
    s01 = jnp.sum(lam[0:1] * lam[1:2], axis=-1, keepdims=True)
    s23 = jnp.sum(lam[2:3] * lam[3:4], axis=-1, keepdims=True)
    return jnp.exp(s01) - jnp.exp(s23) + lam_init


def _prompt_a_kernel(lam_init, q_ref, k_ref, v_ref, lam_ref, gsub_ref, out_ref, m_ref, l_ref, acc_ref):
    i = pl.program_id(1)
    tq = q_ref.shape[1]
    lam_val = _lambda_value(lam_ref, lam_init)
    for hd in range(A_HEADS):
        qms, ksl = [], []
        for s in range(2):
            e = s * A_HEADS + hd
            sl = slice(LANES * (e // 2), LANES * (e // 2 + 1))
            qms.append(_half_mask(q_ref[0, :, sl], e % 2))
            ksl.append(sl)
        vsl = slice(A_DV * hd, A_DV * (hd + 1))

        def qk_fn(kv, qms=qms, ksl=ksl):
            return jnp.concatenate([_dot_nt(qms[s], k_ref[0, kv, ksl[s]]) for s in range(2)], axis=0)

        o = _flash_rows(i, tq, qk_fn, lambda kv, vsl=vsl: v_ref[0, kv, vsl], None, None, m_ref, l_ref, acc_ref)
        d = o[0:tq] - lam_val * o[tq:2 * tq]
        out_ref[0, :, vsl] = (_rms(d, gsub_ref[...]) * (1.0 - lam_init)).astype(BF16)


def _prompt_b_kernel(q1_ref, q2_ref, c_ref, out_ref, m_ref, l_ref, acc_ref):
    i = pl.program_id(1)
    tq = q1_ref.shape[1]
    q1 = jnp.concatenate([q1_ref[0, :, KV_LORA * hd:KV_LORA * (hd + 1)] for hd in range(B_HEADS)], axis=0)
    q2 = jnp.concatenate([q2_ref[0, :, LANES * hd:LANES * (hd + 1)] for hd in range(B_HEADS)], axis=0)

    def qk_fn(kv):
        return _dot_nt(q1, c_ref[0, kv, 0:KV_LORA]) + _dot_nt(q2, c_ref[0, kv, KV_LORA:KV_LORA + LANES])

    o = _flash_rows(i, tq, qk_fn, lambda kv: c_ref[0, kv, 0:KV_LORA], None,
                    (QK_NOPE + QK_ROPE) ** -0.5, m_ref, l_ref, acc_ref)
    for hd in range(B_HEADS):
        out_ref[0, :, KV_LORA * hd:KV_LORA * (hd + 1)] = o[tq * hd:tq * (hd + 1)].astype(BF16)


def _prompt_c_kernel(q_ref, cn_ref, k_ref, v_ref, ct_ref, out_ref, m_ref, l_ref, acc_ref):
    i = pl.program_id(1)
    tq = q_ref.shape[1]
    for pp in range(C_KV_HEADS // 2):
        kvsl = slice(LANES * pp, LANES * (pp + 1))
        res = []
        for half in range(2):
            heads = [4 * pp + 2 * half + r for r in range(2)]
            qs = jnp.concatenate(
                [_half_mask(q_ref[0, :, LANES * (2 * pp + r):LANES * (2 * pp + r + 1)], half) for r in range(2)],
                axis=0)
            cq = jnp.concatenate([cn_ref[0, :, hd:hd + 1] for hd in heads], axis=0)

            def bias_fn(kv, heads=heads, cq=cq):
                ck = jnp.concatenate(
                    [jnp.broadcast_to(ct_ref[0, hd:hd + 1, kv], (tq, tq)) for hd in heads], axis=0)
                return cq - ck

            res.append(_flash_rows(i, tq, lambda kv, qs=qs: _dot_nt(qs, k_ref[0, kv, kvsl]),
                                   lambda kv: v_ref[0, kv, kvsl], bias_fn, None, m_ref, l_ref, acc_ref))
        lane = lax.broadcasted_iota(jnp.int32, (tq, LANES), 1)
        for r in range(2):
            chunk = jnp.where(lane < 64, res[0][tq * r:tq * (r + 1)], res[1][tq * r:tq * (r + 1)])
            out_ref[0, :, LANES * (2 * pp + r):LANES * (2 * pp + r + 1)] = chunk.astype(BF16)


def _prompt_attn(kern, q_list, kv_list, extra, extra_specs, out_w, rows, dv, tq, name):
    bk, t, _ = q_list[0].shape
    qspec = lambda a: pl.BlockSpec((1, tq, a.shape[-1]), lambda b, i: (b, i, 0))
    kvspec = lambda a: pl.BlockSpec((1, t, a.shape[-1]), lambda b, i: (b, 0, 0))
    return pl.pallas_call(
        kern,
        grid=(bk, t // tq),
        in_specs=[qspec(a) for a in q_list] + [kvspec(a) for a in kv_list] + extra_specs,
        out_specs=pl.BlockSpec((1, tq, out_w), lambda b, i: (b, i, 0)),
        out_shape=jax.ShapeDtypeStruct((bk, t, out_w), BF16),
        scratch_shapes=[pltpu.VMEM((rows * tq, 1), F32), pltpu.VMEM((rows * tq, 1), F32),
                        pltpu.VMEM((rows * tq, dv), F32)],
        compiler_params=_cp(("parallel", "arbitrary")),
        name=name,
    )(*q_list, *kv_list, *extra)


def _online_update(s_list, pv_fn, m_ref, l_ref, acc_ref):
    m_prev = m_ref[...]
    m_new = m_prev
    for s in s_list:
        m_new = jnp.maximum(m_new, jnp.max(s, axis=-1, keepdims=True))
    alpha = jnp.exp(m_prev - m_new)
    ps = [jnp.exp(s - m_new) for s in s_list]
    l_new = alpha * l_ref[...]
    for p in ps:
        l_new = l_new + jnp.sum(p, axis=-1, keepdims=True)
    l_ref[...] = l_new
    acc_ref[...] = alpha * acc_ref[...] + pv_fn([p.astype(BF16) for p in ps])
    m_ref[...] = m_new


def _init_softmax(m_ref, l_ref, acc_ref):
    m_ref[...] = jnp.full_like(m_ref, NEG)
    l_ref[...] = jnp.zeros_like(l_ref)
    acc_ref[...] = jnp.zeros_like(acc_ref)


def _new_token_mask(s):
    row = lax.broadcasted_iota(jnp.int32, s.shape, 0) % 8
    col = lax.broadcasted_iota(jnp.int32, s.shape, 1)
    return jnp.where(col <= row, s, NEG)


def _decode_a_kernel(n_pp, lam_init, pt_ref, q_ref, *refs):
    k_refs, v_refs = refs[:n_pp], refs[n_pp:2 * n_pp]
    kn_ref, vn_ref, lam_ref, gsub_ref, out_ref, m_ref, l_ref, acc_ref = refs[2 * n_pp:]
    j = pl.program_id(1)
    last = pl.num_programs(1) - 1

    @pl.when(j == 0)
    def _():
        _init_softmax(m_ref, l_ref, acc_ref)

    q = q_ref[0]

    def pv_fn(v_of):
        def fn(ps):
            outs = []
            for hd in range(A_HEADS):
                o = None
                for p, pr in enumerate(ps):
                    t = _dot(pr[16 * hd:16 * (hd + 1)], v_of(p, hd))
                    o = t if o is None else o + t
                outs.append(o)
            return jnp.concatenate(outs, axis=0)
        return fn

    s_list = [_dot(q, k_refs[p][0, 0].astype(BF16)) for p in range(n_pp)]
    _online_update(s_list,
                   pv_fn(lambda p, hd: v_refs[p][0, 0, pl.ds(hd, PAGE, stride=A_HEADS), :].astype(BF16)),
                   m_ref, l_ref, acc_ref)

    @pl.when(j == last)
    def _():
        s_new = _new_token_mask(_dot(q, kn_ref[0].astype(BF16)))
        _online_update([s_new],
                       pv_fn(lambda p, hd: vn_ref[0, pl.ds(hd, PAGE, stride=A_HEADS), :].astype(BF16)),
                       m_ref, l_ref, acc_ref)
        o = acc_ref[...] / l_ref[...]
        lam_val = _lambda_value(lam_ref, lam_init)
        for hd in range(A_HEADS):
            d = o[16 * hd:16 * hd + 8] - lam_val * o[16 * hd + 8:16 * hd + 16]
            out_ref[0, :, A_DV * hd:A_DV * (hd + 1)] = (
                _rms(d, gsub_ref[...]) * (1.0 - lam_init)).astype(BF16)


def _decode_b_kernel(n_pp, pt_ref, q_ref, *refs):
    c_refs = refs[:n_pp]
    cn_ref, out_ref, m_ref, l_ref, acc_ref = refs[n_pp:]
    j = pl.program_id(1)
    last = pl.num_programs(1) - 1
    scale = (QK_NOPE + QK_ROPE) ** -0.5

    @pl.when(j == 0)
    def _():
        _init_softmax(m_ref, l_ref, acc_ref)

    q = q_ref[0]

    def update(pages, mask):
        s_list = [_dot(q, c) * scale for c in pages]
        if mask:
            s_list = [_new_token_mask(s) for s in s_list]

        def pv(ps):
            o = None
            for pr, c in zip(ps, pages):
                t = _dot_nt(pr, c[0:KV_LORA])
                o = t if o is None else o + t
            return o

        _online_update(s_list, pv, m_ref, l_ref, acc_ref)

    update([c_refs[p][0, 0].astype(BF16) for p in range(n_pp)], False)

    @pl.when(j == last)
    def _():
        update([cn_ref[0].astype(BF16)], True)
        o = acc_ref[...] / l_ref[...]
        for hd in range(B_HEADS):
            out_ref[0, :, KV_LORA * hd:KV_LORA * (hd + 1)] = o[8 * hd:8 * (hd + 1)].astype(BF16)


def _expand_heads(x):
    return jnp.broadcast_to(x[:, None, :], (C_HEADS, 8, x.shape[-1])).reshape(C_HEADS * 8, x.shape[-1])


def _decode_c_kernel(n_pp, pt_ref, q_ref, *refs):
    k_refs, v_refs, lf_refs = refs[:n_pp], refs[n_pp:2 * n_pp], refs[2 * n_pp:3 * n_pp]
    (kn_ref, vn_ref, lfn_ref, tri_ge_ref, tri_le_ref, out_ref,
     m_ref, l_ref, acc_ref, base_ref, bq_ref) = refs[3 * n_pp:]
    j = pl.program_id(1)
    last = pl.num_programs(1) - 1
    q = q_ref[0]

    def pv_of(pages):
        def pv(ps):
            o = None
            for pr, v in zip(ps, pages):
                t = _dot_nt(pr, v)
                o = t if o is None else o + t
            return o
        return pv

    @pl.when(j == 0)
    def _():
        _init_softmax(m_ref, l_ref, acc_ref)
        base_ref[...] = jnp.zeros_like(base_ref)
        cnl = _tri_dot(lfn_ref[0], tri_le_ref[...])
        rep = _expand_heads(cnl)
        row = lax.broadcasted_iota(jnp.int32, rep.shape, 0) % 8
        col = lax.broadcasted_iota(jnp.int32, rep.shape, 1)
        bq = jnp.sum(jnp.where(col == row, rep, 0.0), axis=-1, keepdims=True)
        bq_ref[...] = bq
        s_new = _new_token_mask(_dot(q, kn_ref[0].astype(BF16)) + (bq - rep))
        _online_update([s_new], pv_of([vn_ref[0].astype(BF16)]), m_ref, l_ref, acc_ref)

    lf = jnp.concatenate([lf_refs[p][0, 0] for p in range(n_pp)], axis=0)
    incl = _tri_dot(lf, tri_ge_ref[...])
    excl = incl - lf
    bq = bq_ref[...]
    base = base_ref[...]
    s_list = []
    for p in range(n_pp):
        sl = slice(C_HEADS * p, C_HEADS * (p + 1))
        bias = _expand_heads(base + excl[sl])
        s_list.append(_dot(q, k_refs[p][0, 0].astype(BF16)) + (bq + bias))
        base = base + incl[sl, 0:1]
    base_ref[...] = base
    _online_update(s_list, pv_of([v_refs[p][0, 0].astype(BF16) for p in range(n_pp)]), m_ref, l_ref, acc_ref)

    @pl.when(j == last)
    def _():
        o = acc_ref[...] / l_ref[...]
        lane = lax.broadcasted_iota(jnp.int32, (8, LANES), 1)
        for pp in range(C_KV_HEADS // 2):
            csl = slice(LANES * pp, LANES * (pp + 1))
            for r in range(2):
                h0, h1 = 4 * pp + r, 4 * pp + 2 + r
                chunk = jnp.where(lane < 64, o[8 * h0:8 * (h0 + 1), csl], o[8 * h1:8 * (h1 + 1), csl])
                out_ref[0, :, LANES * (2 * pp + r):LANES * (2 * pp + r + 1)] = chunk.astype(BF16)


def _decode_attn(kern, page_table, q, caches, news, consts, out_w, rows, dv, n_pp, reverse, extra_scratch, name):
    nb, n_pages = page_table.shape
    n_steps = n_pages // n_pp

    def page_spec(c, p):
        def imap(b, j, pt):
            idx = j * n_pp + p
            if reverse:
                idx = n_pages - 1 - idx
            return (0, pt[b, idx], 0, 0)
        return pl.BlockSpec((1, 1) + c.shape[2:], imap)

    in_specs = [pl.BlockSpec((1,) + q.shape[1:], lambda b, j, pt: (b, 0, 0))]
    args = [q]
    for c in caches:
        for p in range(n_pp):
            in_specs.append(page_spec(c, p))
            args.append(c)
    for a in news:
        in_specs.append(pl.BlockSpec((1,) + a.shape[1:], lambda b, j, pt: (b, 0, 0)))
        args.append(a)
    for a in consts:
        in_specs.append(pl.BlockSpec(a.shape, lambda b, j, pt, nd=a.ndim: (0,) * nd))
        args.append(a)
    grid_spec = pltpu.PrefetchScalarGridSpec(
        num_scalar_prefetch=1,
        grid=(nb, n_steps),
        in_specs=in_specs,
        out_specs=pl.BlockSpec((1, 8, out_w), lambda b, j, pt: (b, 0, 0)),
        scratch_shapes=[pltpu.VMEM((rows, 1), F32), pltpu.VMEM((rows, 1), F32), pltpu.VMEM((rows, dv), F32)]
                       + extra_scratch,
    )
    return pl.pallas_call(
        kern,
        grid_spec=grid_spec,
        out_shape=jax.ShapeDtypeStruct((nb, 8, out_w), BF16),
        compiler_params=_cp(("parallel", "arbitrary")),
        name=name,
    )(page_table, *args)


_HEAD_OF_SLOT = [4 * (c // 2) + 2 * half + (c % 2) for c in range(C_HEADS // 2) for half in range(2)]
_SLOT_OF_HEAD = [_HEAD_OF_SLOT.index(h) for h in range(C_HEADS)]


def _lambda_init(layer):
    return 0.8 - 0.6 * math.exp(-0.3 * layer)


def _pad_axis(x, axis, size):
    pad = [(0, 0)] * x.ndim
    pad[axis] = (0, size - x.shape[axis])
    return jnp.pad(x, pad)


def _tok_tile(n, pref):
    tm = min(n, pref)
    assert n % tm == 0 and tm % 8 == 0
    return tm


def _feature_major_page(x):
    return _pad_axis(jnp.swapaxes(x, 1, 2), 2, PAGE)


def kernel(x_prompt, x_sample, cache_a_k, cache_a_v, cache_b_ckv, cache_c_k, cache_c_v, cache_c_logf, state_conv, page_table, norm_gains, w_in_even, w_q_up, w_kv_uk, w_kv_uv, g_q_lat, g_kv_lat, diff_lambda, g_diff_subln, w_out_even, w_in_odd, b_forget, w_out_odd, ffn_w_gate, ffn_w_up, ffn_conv_w, ffn_conv_b, ffn_w_down):
    depth = norm_gains.shape[0]
    bp, tp, d = x_prompt.shape
    bs, ts, _ = x_sample.shape
    n_pages = page_table.shape[1]
    n_pool = cache_a_k.shape[1]
    dff = ffn_w_gate.shape[2]
    assert ts == 8 and cache_a_k.shape[2] == PAGE
    ns = bs * ts
    tm_p, tm_s = _tok_tile(tp, 512), _tok_tile(ns, 512)
    tq = _tok_tile(tp, 256)
    tf = 256
    n_pp = 8 if n_pages % 8 == 0 else n_pages
    assert dff % tf == 0

    pos_p = jnp.arange(tp, dtype=jnp.int32)
    pos_s = n_pages * PAGE + jnp.arange(ts, dtype=jnp.int32)
    tabs_p = (_rope_tables(pos_p, A_ROT), _rope_tables(pos_p, QK_ROPE))
    tabs_s = tuple(tuple(jnp.tile(t, (tm_s // ts, 1)) for t in tabs)
                   for tabs in (_rope_tables(pos_s, A_ROT), _rope_tables(pos_s, QK_ROPE)))

    akT = jnp.transpose(cache_a_k, (0, 1, 3, 4, 5, 2)).reshape(-1, n_pool, 512, PAGE)
    av4 = cache_a_v.reshape(-1, n_pool, PAGE * A_HEADS, A_DV)
    bcT = jnp.swapaxes(cache_b_ckv, 2, 3)
    ckT = jnp.transpose(cache_c_k, (0, 1, 3, 4, 2)).reshape(-1, n_pool, 512, PAGE)
    cvT = jnp.transpose(cache_c_v, (0, 1, 3, 4, 2)).reshape(-1, n_pool, 512, PAGE)
    cfT = jnp.swapaxes(cache_c_logf, 2, 3)

    tri_ge = jnp.tril(jnp.ones((PAGE, PAGE), F32)).astype(BF16)
    tri_le = jnp.triu(jnp.ones((PAGE, PAGE), F32)).astype(BF16)

    hp = x_prompt
    hs = x_sample.reshape(1, ns, d)
    outs_p = {k: [] for k in ("ak", "av", "ckv", "ck", "cv", "cf", "conv")}
    outs_s = {k: [] for k in outs_p}

    for li in range(depth):
        gn = norm_gains[li]
        g0, g1, g2, g3 = (gn[k].reshape(1, d) for k in range(4))
        if li % 2 == 0:
            e = li // 2
            lam_init = _lambda_init(li)
            win = _pad_axis(w_in_even[e], 1, 2304).astype(BF16)
            wq_n = w_q_up[e][:, :, :QK_NOPE].reshape(Q_LORA, B_HEADS * QK_NOPE)
            wq_r = _pad_axis(w_q_up[e][:, :, QK_NOPE:], 2, LANES).reshape(Q_LORA, B_HEADS * LANES)
            wq = jnp.concatenate([wq_n, wq_r], axis=1).astype(BF16)
            wuk = jnp.transpose(w_kv_uk[e], (1, 2, 0)).astype(BF16)
            wuv = jnp.transpose(w_kv_uv[e], (1, 0, 2)).astype(BF16)
            wout = w_out_even[e].astype(BF16)
            gq, gkv = g_q_lat[e].reshape(1, -1), g_kv_lat[e].reshape(1, -1)
            lam, gsub = diff_lambda[e], g_diff_subln[e].reshape(1, -1)

            def even_in(h, tabs, tm):
                return _even_in(h, g0, win, gq, gkv, wq, wuk, tabs[0], tabs[1], tm)

            qa, ka, kab, va, vab, qlat, qrp, ckv, ckvb = even_in(hp, tabs_p, tm_p)
            oa = _prompt_attn(functools.partial(_prompt_a_kernel, lam_init), [qa], [kab, vab], [lam, gsub],
                              [_full_spec(lam.shape), _full_spec(gsub.shape)], 512, 2, A_DV, tq, "prompt_attn_a")
            olat = _prompt_attn(_prompt_b_kernel, [qlat, qrp], [ckvb], [], [], 1024, B_HEADS, KV_LORA, tq,
                                "prompt_attn_b")
            mix_p = (oa, olat)
            outs_p["ak"].append(ka.reshape(bp, tp, 2, A_HEADS, A_DH))
            outs_p["av"].append(va.reshape(bp, tp, A_HEADS, A_DV))
            outs_p["ckv"].append(ckv)

            qa, ka, kab, va, vab, qlat, qrp, ckv, ckvb = even_in(hs, tabs_s, tm_s)
            sel = (jnp.arange(8)[:, None] == jnp.arange(8).reshape(2, 4).T.reshape(8)[None, :])
            qx = jnp.transpose(qa.reshape(bs, ts, 2, A_HEADS, A_DH), (0, 3, 2, 1, 4)).reshape(bs, 8, ts, A_DH)
            q_bd = jnp.where(sel.T[None, :, None, :, None], qx[:, :, :, None, :], jnp.zeros((), BF16))
            q_bd = q_bd.reshape(bs, 8 * ts, 512)
            ka3, va3, ckv3 = ka.reshape(bs, ts, 512), va.reshape(bs, ts, 512), ckv.reshape(bs, ts, 320)
            kn = _feature_major_page(ka3)
            vn = _pad_axis(va3.reshape(bs, ts * A_HEADS, A_DV), 1, PAGE * A_HEADS)
            oa = _decode_attn(functools.partial(_decode_a_kernel, n_pp, lam_init), page_table, q_bd,
                              [akT[e:e + 1], av4[e:e + 1]], [kn, vn], [lam, gsub], 512, 64, A_DV, n_pp, False, [],
                              "decode_attn_a")
            q_abs = jnp.concatenate([qlat.reshape(bs, ts, B_HEADS, KV_LORA),
                                     qrp.reshape(bs, ts, B_HEADS, LANES)[..., :QK_ROPE]], axis=-1)
            q_abs = jnp.swapaxes(q_abs, 1, 2).reshape(bs, B_HEADS * ts, KV_LORA + QK_ROPE)
            olat = _decode_attn(functools.partial(_decode_b_kernel, n_pp), page_table, q_abs,
                                [bcT[e:e + 1]], [_feature_major_page(ckv3)], [], 1024, 32, KV_LORA, n_pp, False, [],
                                "decode_attn_b")
            mix_s = (oa.reshape(1, ns, 512), olat.reshape(1, ns, 1024))
            outs_s["ak"].append(ka.reshape(bs, ts, 2, A_HEADS, A_DH))
            outs_s["av"].append(va.reshape(bs, ts, A_HEADS, A_DV))
            outs_s["ckv"].append(ckv3)

            hp = _mix_out(_even_out_kernel, hp, mix_p, [wuv, wout, g1], tm_p, "even_out_proj")
            hs = _mix_out(_even_out_kernel, hs, mix_s, [wuv, wout, g1], tm_s, "even_out_proj")
        else:
            o = li // 2
            w = w_in_odd[o]
            nq, nk = C_HEADS * C_DH, C_KV_HEADS * C_DH
            wq_perm = w[:, :nq].reshape(d, C_HEADS, C_DH)[:, jnp.array(_HEAD_OF_SLOT), :].reshape(d, nq)
            win = jnp.concatenate([wq_perm, w[:, nq:nq + 2 * nk]], axis=1).astype(BF16)
            wft = jnp.transpose(w[:, nq + 2 * nk:]).astype(BF16)
            bf = b_forget[o].reshape(C_HEADS, 1)
            wout = w_out_odd[o].reshape(C_HEADS, C_DH, d)[jnp.array(_HEAD_OF_SLOT)].reshape(nq, d).astype(BF16)

            q, k, kb, v, vb, lft, ct, cn = _odd_in(hp, g0, win, wft, bf, tm_p)
            op = _prompt_attn(_prompt_c_kernel, [q, cn], [kb, vb], [ct],
                              [pl.BlockSpec((1, C_HEADS, tp), lambda b, i: (b, 0, 0))], 1024, 2, LANES, tq,
                              "prompt_attn_c")
            outs_p["ck"].append(k.reshape(bp, tp, C_KV_HEADS, C_DH))
            outs_p["cv"].append(v.reshape(bp, tp, C_KV_HEADS, C_DH))
            outs_p["cf"].append(jnp.swapaxes(lft, 1, 2))

            q, k, kb, v, vb, lft, ct, cn = _odd_in(hs, g0, win, wft, bf, tm_s)
            qh = q.reshape(bs, ts, C_HEADS, C_DH)[:, :, jnp.array(_SLOT_OF_HEAD), :]
            qh = jnp.swapaxes(qh, 1, 2)
            selc = (jnp.arange(C_HEADS)[:, None] // 2 == jnp.arange(C_KV_HEADS)[None, :])
            q_bd = jnp.where(selc[None, :, None, :, None], qh[:, :, :, None, :], jnp.zeros((), BF16))
            q_bd = q_bd.reshape(bs, C_HEADS * ts, 512)
            k3, v3 = k.reshape(bs, ts, 512), v.reshape(bs, ts, 512)
            lf3 = jnp.swapaxes(lft.reshape(C_HEADS, bs, ts), 0, 1)
            os_ = _decode_attn(functools.partial(_decode_c_kernel, n_pp), page_table, q_bd,
                               [ckT[o:o + 1], cvT[o:o + 1], cfT[o:o + 1]],
                               [_feature_major_page(k3), _feature_major_page(v3), _pad_axis(lf3, 2, PAGE)],
                               [tri_ge, tri_le], 1024, 128, 512, n_pp, True,
                               [pltpu.VMEM((C_HEADS, 1), F32), pltpu.VMEM((C_HEADS * 8, 1), F32)],
                               "decode_attn_c")
            outs_s["ck"].append(k3.reshape(bs, ts, C_KV_HEADS, C_DH))
            outs_s["cv"].append(v3.reshape(bs, ts, C_KV_HEADS, C_DH))
            outs_s["cf"].append(jnp.swapaxes(lf3, 1, 2))

            hp = _mix_out(_odd_out_kernel, hp, (op,), [wout, g1], tm_p, "odd_out_proj")
            hs = _mix_out(_odd_out_kernel, hs, (os_.reshape(1, ns, nq),), [wout, g1], tm_s, "odd_out_proj")

        wg, wu, wd = ffn_w_gate[li].astype(BF16), ffn_w_up[li].astype(BF16), ffn_w_down[li].astype(BF16)
        cw, cb = ffn_conv_w[li], ffn_conv_b[li].reshape(1, dff)
        hp, tail = _ffn(hp, g2, wg, wu, cw, cb, wd, g3, None, tm_p, tf)
        outs_p["conv"].append(tail[:, 8 - (CONV_W - 1):])
        buf = state_conv[li]
        prev2 = _pad_axis(buf, 1, ts).reshape(1, ns, dff)
        prev1 = _pad_axis(buf[:, 1:], 1, ts).reshape(1, ns, dff)
        hs, gate = _ffn(hs, g2, wg, wu, cw, cb, wd, g3, (prev2, prev1), tm_s, tf)
        outs_s["conv"].append(gate.reshape(bs, ts, dff)[:, ts - (CONV_W - 1):])

    st = lambda xs: jnp.stack(xs)
    res = [hp, hs.reshape(bs, ts, d)]
    for key in ("ak", "av", "ckv", "ck", "cv", "cf", "conv"):
        res += [st(outs_p[key]), st(outs_s[key])]
    return tuple(res)
```

```python
import functools
import math

import jax
import jax.numpy as jnp
from jax import lax
from jax.experimental import pallas as pl
from jax.experimental.pallas import tpu as pltpu

F32 = jnp.float32
BF16 = jnp.bfloat16

EPS = 1e-6
ROPE_THETA = 500000.0
A_HEADS, A_DH, A_DV, A_ROT = 4, 64, 128, 16
B_HEADS, Q_LORA, KV_LORA, QK_NOPE, QK_ROPE, B_DV = 4, 384, 256, 128, 64, 128
C_HEADS, C_KV_HEADS, C_DH = 16, 8, 64
CONV_W = 3
PAGE = 128
LANES = 128
NEG = -1e30
VMEM_LIMIT_BYTES = 56 * 1024 * 1024

_NT = (((1,), (1,)), ((), ()))


def _cp(sem):
    return pltpu.CompilerParams(dimension_semantics=sem, vmem_limit_bytes=VMEM_LIMIT_BYTES)


def _rms(x, g):
    return x * lax.rsqrt(jnp.mean(x * x, axis=-1, keepdims=True) + EPS) * g


def _dot(a, b):
    return jnp.dot(a, b, preferred_element_type=F32)


def _dot_nt(a, b):
    return lax.dot_general(a, b, _NT, preferred_element_type=F32)


def _rope_chunk(x, c, sl, sr, shift):
    return x * c + pltpu.roll(x, LANES - shift, 1) * sl + pltpu.roll(x, shift, 1) * sr


def _rope_tables(pos, rot):
    half = rot // 2
    inv = jnp.power(jnp.float32(ROPE_THETA), -jnp.arange(half, dtype=F32) / half)
    ang = pos.astype(F32)[:, None] * inv[None, :]
    cos, sin = jnp.cos(ang), jnp.sin(ang)
    t = pos.shape[0]
    one = jnp.ones((t, 64 - rot), F32)
    z_rest = jnp.zeros((t, 64 - rot), F32)
    z_half = jnp.zeros((t, half), F32)
    c = jnp.concatenate([cos, cos, one], axis=1)
    sl = jnp.concatenate([-sin, z_half, z_rest], axis=1)
    sr = jnp.concatenate([z_half, sin, z_rest], axis=1)
    return tuple(jnp.tile(a, (1, 2)) for a in (c, sl, sr))


def _even_in_kernel(h_ref, g0_ref, win_ref, gq_ref, gkv_ref, wq_ref, wuk_ref,
                    ca_ref, sla_ref, sra_ref, cb_ref, slb_ref, srb_ref,
                    qa_ref, ka_ref, kab_ref, va_ref, vab_ref, qlat_ref, qrp_ref, ckv_ref, ckvb_ref):
    hn = _rms(h_ref[0], g0_ref[...]).astype(BF16)
    z = _dot(hn, win_ref[...])
    ca, sla, sra = ca_ref[...], sla_ref[...], sra_ref[...]
    cb, slb, srb = cb_ref[...], slb_ref[...], srb_ref[...]
    for j in range(4):
        sl = slice(LANES * j, LANES * (j + 1))
        qa_ref[0, :, sl] = (_rope_chunk(z[:, sl], ca, sla, sra, A_ROT // 2) * (A_DH ** -0.5)).astype(BF16)
        kr = _rope_chunk(z[:, 512 + LANES * j:512 + LANES * (j + 1)], ca, sla, sra, A_ROT // 2)
        ka_ref[0, :, sl] = kr
        kab_ref[0, :, sl] = kr.astype(BF16)
    va = z[:, 1024:1536]
    va_ref[0] = va
    vab_ref[0] = va.astype(BF16)
    cqn = _rms(z[:, 1536:1920], gq_ref[...]).astype(BF16)
    q2 = _dot(cqn, wq_ref[...])
    for h in range(B_HEADS):
        qn = q2[:, LANES * h:LANES * (h + 1)].astype(BF16)
        qlat_ref[0, :, KV_LORA * h:KV_LORA * (h + 1)] = _dot(qn, wuk_ref[h]).astype(BF16)
        qr = _rope_chunk(q2[:, 512 + LANES * h:512 + LANES * (h + 1)], cb, slb, srb, QK_ROPE // 2)
        qrp_ref[0, :, LANES * h:LANES * (h + 1)] = qr.astype(BF16)
    cn = _rms(z[:, 1920:2176], gkv_ref[...])
    krr = _rope_chunk(z[:, 2176:2304], cb, slb, srb, QK_ROPE // 2)
    ckv_ref[0, :, 0:KV_LORA] = cn
    ckv_ref[0, :, KV_LORA:KV_LORA + QK_ROPE] = krr[:, :QK_ROPE]
    ckvb_ref[0, :, 0:KV_LORA] = cn.astype(BF16)
    ckvb_ref[0, :, KV_LORA:KV_LORA + LANES] = krr.astype(BF16)


def _tab_spec(tm, n_tab_tiles):
    return pl.BlockSpec((tm, LANES), lambda b, i: (i % n_tab_tiles, 0))


def _full_spec(shape):
    nd = len(shape)
    return pl.BlockSpec(shape, lambda b, i: (0,) * nd)


def _even_in(h, g0, win, gq, gkv, wq, wuk, tabs_a, tabs_b, tm):
    bk, tk, d = h.shape
    n_tab = tabs_a[0].shape[0] // tm
    tok = lambda w: pl.BlockSpec((1, tm, w), lambda b, i: (b, i, 0))
    out_shapes = [
        jax.ShapeDtypeStruct((bk, tk, 512), BF16),
        jax.ShapeDtypeStruct((bk, tk, 512), F32),
        jax.ShapeDtypeStruct((bk, tk, 512), BF16),
        jax.ShapeDtypeStruct((bk, tk, 512), F32),
        jax.ShapeDtypeStruct((bk, tk, 512), BF16),
        jax.ShapeDtypeStruct((bk, tk, 1024), BF16),
        jax.ShapeDtypeStruct((bk, tk, 512), BF16),
        jax.ShapeDtypeStruct((bk, tk, 320), F32),
        jax.ShapeDtypeStruct((bk, tk, 384), BF16),
    ]
    return pl.pallas_call(
        _even_in_kernel,
        grid=(bk, tk // tm),
        in_specs=[tok(d), _full_spec(g0.shape), _full_spec(win.shape), _full_spec(gq.shape),
                  _full_spec(gkv.shape), _full_spec(wq.shape), _full_spec(wuk.shape)]
                 + [_tab_spec(tm, n_tab)] * 6,
        out_specs=[tok(s.shape[-1]) for s in out_shapes],
        out_shape=out_shapes,
        compiler_params=_cp(("parallel", "parallel")),
        name="even_in_proj",
    )(h, g0, win, gq, gkv, wq, wuk, *tabs_a, *tabs_b)


def _split3(x):
    hi = x.astype(BF16)
    r = x - hi.astype(F32)
    mid = r.astype(BF16)
    lo = (r - mid.astype(F32)).astype(BF16)
    return hi, mid, lo


def _tri_dot(x, tri):
    n = x.shape[0]
    hi, mid, lo = _split3(x)
    y = _dot(jnp.concatenate([hi, mid, lo], axis=0), tri)
    return y[0:n] + y[n:2 * n] + y[2 * n:3 * n]


def _log_sigmoid(x):
    return -(jnp.maximum(-x, 0.0) + jnp.log1p(jnp.exp(-jnp.abs(x))))


def _odd_in_kernel(h_ref, g0_ref, win_ref, wft_ref, bf_ref, tri_ref,
                   q_ref, k_ref, kb_ref, v_ref, vb_ref, lft_ref, ct_ref, cn_ref, carry_ref):
    i = pl.program_id(1)
    hn = _rms(h_ref[0], g0_ref[...]).astype(BF16)
    z = _dot(hn, win_ref[...])
    q_ref[0] = (z[:, 0:1024] * (C_DH ** -0.5)).astype(BF16)
    k = z[:, 1024:1536]
    k_ref[0] = k
    kb_ref[0] = k.astype(BF16)
    v = z[:, 1536:2048]
    v_ref[0] = v
    vb_ref[0] = v.astype(BF16)
    ft = _dot_nt(wft_ref[...], hn)
    lft = _log_sigmoid(ft + bf_ref[...])
    lft_ref[0] = lft

    @pl.when(i == 0)
    def _():
        carry_ref[...] = jnp.zeros_like(carry_ref)

    c = _tri_dot(lft, tri_ref[...]) + carry_ref[...]
    carry_ref[...] = c[:, -1:]
    ct_ref[0] = c
    c128 = jnp.concatenate([c, jnp.zeros((LANES - C_HEADS, c.shape[1]), F32)], axis=0)
    cn_ref[0] = jnp.transpose(c128)[:, 0:C_HEADS]


def _odd_in(h, g0, win, wft, bf, tm):
    bk, tk, d = h.shape
    tri = jnp.triu(jnp.ones((tm, tm), F32)).astype(BF16)
    tok = lambda w: pl.BlockSpec((1, tm, w), lambda b, i: (b, i, 0))
    tokt = pl.BlockSpec((1, C_HEADS, tm), lambda b, i: (b, 0, i))
    out_shapes = [
        jax.ShapeDtypeStruct((bk, tk, 1024), BF16),
        jax.ShapeDtypeStruct((bk, tk, 512), F32),
        jax.ShapeDtypeStruct((bk, tk, 512), BF16),
        jax.ShapeDtypeStruct((bk, tk, 512), F32),
        jax.ShapeDtypeStruct((bk, tk, 512), BF16),
        jax.ShapeDtypeStruct((bk, C_HEADS, tk), F32),
        jax.ShapeDtypeStruct((bk, C_HEADS, tk), F32),
        jax.ShapeDtypeStruct((bk, tk, C_HEADS), F32),
    ]
    return pl.pallas_call(
        _odd_in_kernel,
        grid=(bk, tk // tm),
        in_specs=[tok(d), _full_spec(g0.shape), _full_spec(win.shape), _full_spec(wft.shape),
                  _full_spec(bf.shape), _full_spec(tri.shape)],
        out_specs=[tok(1024), tok(512), tok(512), tok(512), tok(512), tokt, tokt, tok(C_HEADS)],
        out_shape=out_shapes,
        scratch_shapes=[pltpu.VMEM((C_HEADS, 1), F32)],
        compiler_params=_cp(("parallel", "arbitrary")),
        name="odd_in_proj",
    )(h, g0, win, wft, bf, tri)


def _even_out_kernel(h_ref, oa_ref, olat_ref, wuv_ref, wout_ref, g1_ref, out_ref):
    parts = [oa_ref[0]]
    for hd in range(B_HEADS):
        parts.append(_dot(olat_ref[0, :, KV_LORA * hd:KV_LORA * (hd + 1)], wuv_ref[hd]).astype(BF16))
    y = _dot(jnp.concatenate(parts, axis=1), wout_ref[...])
    out_ref[0] = h_ref[0] + _rms(y, g1_ref[...])


def _odd_out_kernel(h_ref, o_ref, wout_ref, g1_ref, out_ref):
    y = _dot(o_ref[0], wout_ref[...])
    out_ref[0] = h_ref[0] + _rms(y, g1_ref[...])


def _mix_out(kern, h, acts, consts, tm, name):
    bk, tk, d = h.shape
    tok = lambda w: pl.BlockSpec((1, tm, w), lambda b, i: (b, i, 0))
    return pl.pallas_call(
        kern,
        grid=(bk, tk // tm),
        in_specs=[tok(d)] + [tok(a.shape[-1]) for a in acts] + [_full_spec(c.shape) for c in consts],
        out_specs=tok(d),
        out_shape=jax.ShapeDtypeStruct(h.shape, F32),
        compiler_params=_cp(("parallel", "parallel")),
        name=name,
    )(h, *acts, *consts)


def _ffn_kernel(short_seq, *refs):
    if short_seq:
        (h_ref, g2_ref, wg_ref, wu_ref, cw_ref, cb_ref, wd_ref, g3_ref, pa_ref, pb_ref,
         out_ref, gate_ref, hn_ref, acc_ref) = refs
    else:
        (h_ref, g2_ref, wg_ref, wu_ref, cw_ref, cb_ref, wd_ref, g3_ref,
         out_ref, tail_ref, hn_ref, acc_ref, carry_ref) = refs
    i = pl.program_id(1)
    c = pl.program_id(2)

    @pl.when(c == 0)
    def _():
        hn_ref[...] = _rms(h_ref[0], g2_ref[...]).astype(BF16)
        acc_ref[...] = jnp.zeros_like(acc_ref)

    hn = hn_ref[...]
    g = _dot(hn, wg_ref[...])
    u = _dot(hn, wu_ref[...])
    tm = g.shape[0]
    row = lax.broadcasted_iota(jnp.int32, g.shape, 0)
    r1 = pltpu.roll(g, 1, 0)
    r2 = pltpu.roll(g, 2, 0)
    if short_seq:
        t = row % 8
        gm1 = jnp.where(t == 0, pb_ref[0], r1)
        gm2 = jnp.where(t < 2, pa_ref[0], r2)
        gate_ref[0] = g
    else:
        top = jnp.where(i == 0, 0.0, carry_ref[c])
        gm1 = jnp.where(row == 0, top[7:8], r1)
        gm2 = jnp.where(row == 0, top[6:7], jnp.where(row == 1, top[7:8], r2))
        carry_ref[c] = g[tm - 8:tm]
        tail_ref[0] = g[tm - 8:tm]
    cw = cw_ref[...]
    gc = cb_ref[...] + ((cw[0:1] * gm2 + cw[1:2] * gm1) + cw[2:3] * g)
    act = (jax.nn.silu(gc) * u).astype(BF16)
    acc_ref[...] += _dot(act, wd_ref[...])

    @pl.when(c == pl.num_programs(2) - 1)
    def _():
        out_ref[0] = h_ref[0] + _rms(acc_ref[...], g3_ref[...])


def _ffn(h, g2, wg, wu, cw, cb, wd, g3, prev, tm, tf):
    bk, tk, d = h.shape
    dff = wg.shape[1]
    nc = dff // tf
    short_seq = prev is not None
    tok = pl.BlockSpec((1, tm, d), lambda b, i, c: (b, i, 0))
    const = lambda shape: pl.BlockSpec(shape, lambda b, i, c: (0,) * len(shape))
    in_specs = [tok, const(g2.shape),
                pl.BlockSpec((d, tf), lambda b, i, c: (0, c)),
                pl.BlockSpec((d, tf), lambda b, i, c: (0, c)),
                pl.BlockSpec((CONV_W, tf), lambda b, i, c: (0, c)),
                pl.BlockSpec((1, tf), lambda b, i, c: (0, c)),
                pl.BlockSpec((tf, d), lambda b, i, c: (c, 0)),
                const(g3.shape)]
    args = [h, g2, wg, wu, cw, cb, wd, g3]
    scratch = [pltpu.VMEM((tm, d), BF16), pltpu.VMEM((tm, d), F32)]
    if short_seq:
        in_specs += [pl.BlockSpec((1, tm, tf), lambda b, i, c: (b, i, c))] * 2
        args += list(prev)
        out_specs = [tok, pl.BlockSpec((1, tm, tf), lambda b, i, c: (b, i, c))]
        out_shape = [jax.ShapeDtypeStruct(h.shape, F32), jax.ShapeDtypeStruct((bk, tk, dff), F32)]
    else:
        out_specs = [tok, pl.BlockSpec((1, 8, tf), lambda b, i, c: (b, 0, c))]
        out_shape = [jax.ShapeDtypeStruct(h.shape, F32), jax.ShapeDtypeStruct((bk, 8, dff), F32)]
        scratch.append(pltpu.VMEM((nc, 8, tf), F32))
    return pl.pallas_call(
        functools.partial(_ffn_kernel, short_seq),
        grid=(bk, tk // tm, nc),
        in_specs=in_specs,
        out_specs=out_specs,
        out_shape=out_shape,
        scratch_shapes=scratch,
        compiler_params=_cp(("parallel", "arbitrary", "arbitrary")),
        name="conv_ffn",
    )(*args)


def _flash_cols(i, tq, chains):
    for c in chains:
        c["m"][...] = jnp.full_like(c["m"], NEG)
        c["l"][...] = jnp.zeros_like(c["l"])
        c["acc"][...] = jnp.zeros_like(c["acc"])

    def step(j, masked):
        kv = pl.ds(pl.multiple_of(j * tq, tq), tq)
        for c in chains:
            s = c["qk"](kv)
            if c["scale"] is not None:
                s = s * c["scale"]
            if c["bias"] is not None:
                s = s + c["bias"](kv)
            if masked:
                key = lax.broadcasted_iota(jnp.int32, s.shape, 0)
                qpos = lax.broadcasted_iota(jnp.int32, s.shape, 1) % tq
                s = jnp.where(key <= qpos, s, NEG)
            m_prev = c["m"][...]
            m_new = jnp.maximum(m_prev, jnp.max(s, axis=0, keepdims=True))
            alpha = jnp.exp(m_prev - m_new)
            p = jnp.exp(s - m_new)
            c["l"][...] = alpha * c["l"][...] + jnp.sum(p, axis=0, keepdims=True)
            c["acc"][...] = alpha * c["acc"][...] + _dot(c["vt"](kv), p.astype(BF16))
            c["m"][...] = m_new

    def body(j, carry):
        step(j, False)
        return carry

    lax.fori_loop(0, i, body, 0)
    step(i, True)
    return [c["acc"][...] / c["l"][...] for c in chains]


def _transpose_bf16(x):
    return jnp.transpose(x.astype(F32)).astype(BF16)


def _fill_transposed(i, tq, src_fn, dst_ref):
    n_tiles = dst_ref.shape[1] // tq

    @pl.when(i == 0)
    def _():
        def body(c, carry):
            rows = pl.ds(pl.multiple_of(c * tq, tq), tq)
            dst_ref[:, rows] = _transpose_bf16(src_fn(rows))
            return carry

        lax.fori_loop(0, n_tiles, body, 0)


def _chain_scratch(dv, r):
    return [pltpu.VMEM((1, r), F32), pltpu.VMEM((1, r), F32), pltpu.VMEM((dv, r), F32)]


def _lambda_value(lam_ref, lam_init):
    lam = lam_ref[...]
    s01 = jnp.sum(lam[0:1] * lam[1:2], axis=-1, keepdims=True)
    s23 = jnp.sum(lam[2:3] * lam[3:4], axis=-1, keepdims=True)
    return jnp.exp(s01) - jnp.exp(s23) + lam_init


def _half_rows(blk, half):
    z = jnp.zeros_like(blk)
    return jnp.concatenate([blk, z] if half == 0 else [z, blk], axis=0)


def _prompt_a_kernel(lam_init, q_ref, k_ref, v_ref, lam_ref, gsub_ref, out_ref, vt_ref, *st):
    i = pl.program_id(1)
    tq = q_ref.shape[1]
    _fill_transposed(i, tq, lambda rows: v_ref[0, rows, :], vt_ref)
    qt = _transpose_bf16(q_ref[0])
    lam_val = _lambda_value(lam_ref, lam_init)
    for pair in range(A_HEADS // 2):
        chains = []
        for hd in (2 * pair, 2 * pair + 1):
            slots = [s * A_HEADS + hd for s in range(2)]
            qts = [_half_rows(qt[64 * e:64 * (e + 1)], e % 2) for e in slots]
            ksl = [slice(LANES * (e // 2), LANES * (e // 2 + 1)) for e in slots]

            def qk(kv, qts=qts, ksl=ksl):
                return jnp.concatenate([_dot(k_ref[0, kv, ksl[s]], qts[s]) for s in range(2)], axis=1)

            def vt(kv, hd=hd):
                return vt_ref[A_DV * hd:A_DV * (hd + 1), kv]

            chains.append(dict(qk=qk, vt=vt, scale=None, bias=None,
                               m=st[3 * hd], l=st[3 * hd + 1], acc=st[3 * hd + 2]))
        for hd, ot in zip((2 * pair, 2 * pair + 1), _flash_cols(i, tq, chains)):
            d = jnp.transpose(ot[:, 0:tq] - lam_val * ot[:, tq:2 * tq])
            out_ref[0, :, A_DV * hd:A_DV * (hd + 1)] = (_rms(d, gsub_ref[...]) * (1.0 - lam_init)).astype(BF16)


def _prompt_b_kernel(q1_ref, q2_ref, c_ref, out_ref, vt_ref, m_ref, l_ref, acc_ref):
    i = pl.program_id(1)
    tq = q1_ref.shape[1]
    _fill_transposed(i, tq, lambda rows: c_ref[0, rows, 0:KV_LORA], vt_ref)
    q1t = _transpose_bf16(q1_ref[0])
    q2t = _transpose_bf16(q2_ref[0])
    q1s = jnp.concatenate([q1t[KV_LORA * hd:KV_LORA * (hd + 1)] for hd in range(B_HEADS)], axis=1)
    q2s = jnp.concatenate([q2t[LANES * hd:LANES * (hd + 1)] for hd in range(B_HEADS)], axis=1)

    def qk(kv):
        return _dot(c_ref[0, kv, 0:KV_LORA], q1s) + _dot(c_ref[0, kv, KV_LORA:KV_LORA + LANES], q2s)

    chain = dict(qk=qk, vt=lambda kv: vt_ref[:, kv], scale=(QK_NOPE + QK_ROPE) ** -0.5, bias=None,
                 m=m_ref, l=l_ref, acc=acc_ref)
    (ot,) = _flash_cols(i, tq, [chain])
    for hd in range(B_HEADS):
        out_ref[0, :, KV_LORA * hd:KV_LORA * (hd + 1)] = jnp.transpose(ot[:, tq * hd:tq * (hd + 1)]).astype(BF16)


def _prompt_c_kernel(q_ref, ct_ref, k_ref, v_ref, cn_ref, out_ref, vt_ref, *st):
    i = pl.program_id(1)
    tq = q_ref.shape[1]
    _fill_transposed(i, tq, lambda rows: v_ref[0, rows, :], vt_ref)
    qt = _transpose_bf16(q_ref[0])
    for pp in range(C_KV_HEADS // 2):
        chains = []
        for half in range(2):
            heads = [4 * pp + 2 * half + r for r in range(2)]
            blk = jnp.concatenate(
                [qt[LANES * (2 * pp + r) + 64 * half:LANES * (2 * pp + r) + 64 * (half + 1)] for r in range(2)],
                axis=1)
            qs = _half_rows(blk, half)
            cq = [ct_ref[0, hd:hd + 1, :] for hd in heads]

            def bias(kv, heads=heads, cq=cq):
                return jnp.concatenate([cq[r] - cn_ref[0, kv, hd:hd + 1] for r, hd in enumerate(heads)], axis=1)

            def qk(kv, qs=qs):
                return _dot(k_ref[0, kv, LANES * pp:LANES * (pp + 1)], qs)

            def vt(kv, g=2 * pp + half):
                return vt_ref[C_DH * g:C_DH * (g + 1), kv]

            chains.append(dict(qk=qk, vt=vt, scale=None, bias=bias,
                               m=st[3 * half], l=st[3 * half + 1], acc=st[3 * half + 2]))
        ot = jnp.concatenate(_flash_cols(i, tq, chains), axis=0)
        for r in range(2):
            out_ref[0, :, LANES * (2 * pp + r):LANES * (2 * pp + r + 1)] = jnp.transpose(
                ot[:, tq * r:tq * (r + 1)]).astype(BF16)


def _prompt_attn(kern, args, in_specs, bk, t, out_w, vt_rows, chain_shapes, tq, name):
    scratch = [pltpu.VMEM((vt_rows, t), BF16)]
    for dv, r in chain_shapes:
        scratch += _chain_scratch(dv, r)
    return pl.pallas_call(
        kern,
        grid=(bk, t // tq),
        in_specs=in_specs,
        out_specs=pl.BlockSpec((1, tq, out_w), lambda b, i: (b, i, 0)),
        out_shape=jax.ShapeDtypeStruct((bk, t, out_w), BF16),
        scratch_shapes=scratch,
        compiler_params=_cp(("parallel", "arbitrary")),
        name=name,
    )(*args)


def _q_tile_spec(a, tq):
    return pl.BlockSpec((1, tq, a.shape[-1]), lambda b, i: (b, i, 0))


def _seq_spec(a):
    return pl.BlockSpec((1,) + a.shape[1:], lambda b, i: (b, 0, 0))


def _online_update(s_list, pv_fn, m_ref, l_ref, acc_ref):
    m_prev = m_ref[...]
    m_new = m_prev
    for s in s_list:
        m_new = jnp.maximum(m_new, jnp.max(s, axis=-1, keepdims=True))
    alpha = jnp.exp(m_prev - m_new)
    ps = [jnp.exp(s - m_new) for s in s_list]
    l_new = alpha * l_ref[...]
    for p in ps:
        l_new = l_new + jnp.sum(p, axis=-1, keepdims=True)
    l_ref[...] = l_new
    acc_ref[...] = alpha * acc_ref[...] + pv_fn([p.astype(BF16) for p in ps])
    m_ref[...] = m_new


def _init_softmax(m_ref, l_ref, acc_ref):
    m_ref[...] = jnp.full_like(m_ref, NEG)
    l_ref[...] = jnp.zeros_like(l_ref)
    acc_ref[...] = jnp.zeros_like(acc_ref)


def _new_token_mask(s):
    row = lax.broadcasted_iota(jnp.int32, s.shape, 0) % 8
    col = lax.broadcasted_iota(jnp.int32, s.shape, 1)
    return jnp.where(col <= row, s, NEG)


def _decode_a_kernel(n_pp, lam_init, pt_ref, q_ref, *refs):
    k_refs, v_refs = refs[:n_pp], refs[n_pp:2 * n_pp]
    kn_ref, vn_ref, lam_ref, gsub_ref, out_ref, m_ref, l_ref, acc_ref = refs[2 * n_pp:]
    j = pl.program_id(1)
    last = pl.num_programs(1) - 1

    @pl.when(j == 0)
    def _():
        _init_softmax(m_ref, l_ref, acc_ref)

    q = q_ref[0]

    def pv_fn(v_of):
        def fn(ps):
            outs = []
            for hd in range(A_HEADS):
                o = None
                for p, pr in enumerate(ps):
                    t = _dot(pr[16 * hd:16 * (hd + 1)], v_of(p, hd))
                    o = t if o is None else o + t
                outs.append(o)
            return jnp.concatenate(outs, axis=0)
        return fn

    s_list = [_dot(q, k_refs[p][0, 0].astype(BF16)) for p in range(n_pp)]
    _online_update(s_list,
                   pv_fn(lambda p, hd: v_refs[p][0, 0, pl.ds(hd, PAGE, stride=A_HEADS), :].astype(BF16)),
                   m_ref, l_ref, acc_ref)

    @pl.when(j == last)
    def _():
        s_new = _new_token_mask(_dot(q, kn_ref[0].astype(BF16)))
        _online_update([s_new],
                       pv_fn(lambda p, hd: vn_ref[0, pl.ds(hd, PAGE, stride=A_HEADS), :].astype(BF16)),
                       m_ref, l_ref, acc_ref)
        o = acc_ref[...] / l_ref[...]
        lam_val = _lambda_value(lam_ref, lam_init)
        for hd in range(A_HEADS):
            d = o[16 * hd:16 * hd + 8] - lam_val * o[16 * hd + 8:16 * hd + 16]
            out_ref[0, :, A_DV * hd:A_DV * (hd + 1)] = (
                _rms(d, gsub_ref[...]) * (1.0 - lam_init)).astype(BF16)


def _decode_b_kernel(n_pp, pt_ref, q_ref, *refs):
    c_refs = refs[:n_pp]
    cn_ref, out_ref, m_ref, l_ref, acc_ref = refs[n_pp:]
    j = pl.program_id(1)
    last = pl.num_programs(1) - 1
    scale = (QK_NOPE + QK_ROPE) ** -0.5

    @pl.when(j == 0)
    def _():
        _init_softmax(m_ref, l_ref, acc_ref)

    q = q_ref[0]

    def update(pages, mask):
        s_list = [_dot(q, c) * scale for c in pages]
        if mask:
            s_list = [_new_token_mask(s) for s in s_list]

        def pv(ps):
            o = None
            for pr, c in zip(ps, pages):
                t = _dot_nt(pr, c[0:KV_LORA])
                o = t if o is None else o + t
            return o

        _online_update(s_list, pv, m_ref, l_ref, acc_ref)

    update([c_refs[p][0, 0].astype(BF16) for p in range(n_pp)], False)

    @pl.when(j == last)
    def _():
        update([cn_ref[0].astype(BF16)], True)
        o = acc_ref[...] / l_ref[...]
        for hd in range(B_HEADS):
            out_ref[0, :, KV_LORA * hd:KV_LORA * (hd + 1)] = o[8 * hd:8 * (hd + 1)].astype(BF16)


def _expand_heads(x):
    return jnp.broadcast_to(x[:, None, :], (C_HEADS, 8, x.shape[-1])).reshape(C_HEADS * 8, x.shape[-1])


def _decode_c_kernel(n_pp, pt_ref, q_ref, *refs):
    k_refs, v_refs, lf_refs = refs[:n_pp], refs[n_pp:2 * n_pp], refs[2 * n_pp:3 * n_pp]
    (kn_ref, vn_ref, lfn_ref, tri_ge_ref, tri_le_ref, out_ref,
     m_ref, l_ref, acc_ref, base_ref, bq_ref) = refs[3 * n_pp:]
    j = pl.program_id(1)
    last = pl.num_programs(1) - 1
    q = q_ref[0]

    def pv_of(pages):
        def pv(ps):
            o = None
            for pr, v in zip(ps, pages):
                t = _dot_nt(pr, v)
                o = t if o is None else o + t
            return o
        return pv

    @pl.when(j == 0)
    def _():
        _init_softmax(m_ref, l_ref, acc_ref)
        base_ref[...] = jnp.zeros_like(base_ref)
        cnl = _tri_dot(lfn_ref[0], tri_le_ref[...])
        rep = _expand_heads(cnl)
        row = lax.broadcasted_iota(jnp.int32, rep.shape, 0) % 8
        col = lax.broadcasted_iota(jnp.int32, rep.shape, 1)
        bq = jnp.sum(jnp.where(col == row, rep, 0.0), axis=-1, keepdims=True)
        bq_ref[...] = bq
        s_new = _new_token_mask(_dot(q, kn_ref[0].astype(BF16)) + (bq - rep))
        _online_update([s_new], pv_of([vn_ref[0].astype(BF16)]), m_ref, l_ref, acc_ref)

    lf = jnp.concatenate([lf_refs[p][0, 0] for p in range(n_pp)], axis=0)
    incl = _tri_dot(lf, tri_ge_ref[...])
    excl = incl - lf
    bq = bq_ref[...]
    base = base_ref[...]
    s_list = []
    for p in range(n_pp):
        sl = slice(C_HEADS * p, C_HEADS * (p + 1))
        bias = _expand_heads(base + excl[sl])
        s_list.append(_dot(q, k_refs[p][0, 0].astype(BF16)) + (bq + bias))
        base = base + incl[sl, 0:1]
    base_ref[...] = base
    _online_update(s_list, pv_of([v_refs[p][0, 0].astype(BF16) for p in range(n_pp)]), m_ref, l_ref, acc_ref)

    @pl.when(j == last)
    def _():
        o = acc_ref[...] / l_ref[...]
        lane = lax.broadcasted_iota(jnp.int32, (8, LANES), 1)
        for pp in range(C_KV_HEADS // 2):
            csl = slice(LANES * pp, LANES * (pp + 1))
            for r in range(2):
                h0, h1 = 4 * pp + r, 4 * pp + 2 + r
                chunk = jnp.where(lane < 64, o[8 * h0:8 * (h0 + 1), csl], o[8 * h1:8 * (h1 + 1), csl])
                out_ref[0, :, LANES * (2 * pp + r):LANES * (2 * pp + r + 1)] = chunk.astype(BF16)


def _decode_attn(kern, page_table, q, caches, news, consts, out_w, rows, dv, n_pp, reverse, extra_scratch, name):
    nb, n_pages = page_table.shape
    n_steps = n_pages // n_pp

    def page_spec(c, p):
        def imap(b, j, pt):
            idx = j * n_pp + p
            if reverse:
                idx = n_pages - 1 - idx
            return (0, pt[b, idx], 0, 0)
        return pl.BlockSpec((1, 1) + c.shape[2:], imap)

    in_specs = [pl.BlockSpec((1,) + q.shape[1:], lambda b, j, pt: (b, 0, 0))]
    args = [q]
    for c in caches:
        for p in range(n_pp):
            in_specs.append(page_spec(c, p))
            args.append(c)
    for a in news:
        in_specs.append(pl.BlockSpec((1,) + a.shape[1:], lambda b, j, pt: (b, 0, 0)))
        args.append(a)
    for a in consts:
        in_specs.append(pl.BlockSpec(a.shape, lambda b, j, pt, nd=a.ndim: (0,) * nd))
        args.append(a)
    grid_spec = pltpu.PrefetchScalarGridSpec(
        num_scalar_prefetch=1,
        grid=(nb, n_steps),
        in_specs=in_specs,
        out_specs=pl.BlockSpec((1, 8, out_w), lambda b, j, pt: (b, 0, 0)),
        scratch_shapes=[pltpu.VMEM((rows, 1), F32), pltpu.VMEM((rows, 1), F32), pltpu.VMEM((rows, dv), F32)]
                       + extra_scratch,
    )
    return pl.pallas_call(
        kern,
        grid_spec=grid_spec,
        out_shape=jax.ShapeDtypeStruct((nb, 8, out_w), BF16),
        compiler_params=_cp(("parallel", "arbitrary")),
        name=name,
    )(page_table, *args)


_HEAD_OF_SLOT = [4 * (c // 2) + 2 * half + (c % 2) for c in range(C_HEADS // 2) for half in range(2)]
_SLOT_OF_HEAD = [_HEAD_OF_SLOT.index(h) for h in range(C_HEADS)]


def _lambda_init(layer):
    return 0.8 - 0.6 * math.exp(-0.3 * layer)


def _pad_axis(x, axis, size):
    pad = [(0, 0)] * x.ndim
    pad[axis] = (0, size - x.shape[axis])
    return jnp.pad(x, pad)


def _tok_tile(n, pref):
    tm = min(n, pref)
    assert n % tm == 0 and tm % 8 == 0
    return tm


def _feature_major_page(x):
    return _pad_axis(jnp.swapaxes(x, 1, 2), 2, PAGE)


def kernel(x_prompt, x_sample, cache_a_k, cache_a_v, cache_b_ckv, cache_c_k, cache_c_v, cache_c_logf, state_conv, page_table, norm_gains, w_in_even, w_q_up, w_kv_uk, w_kv_uv, g_q_lat, g_kv_lat, diff_lambda, g_diff_subln, w_out_even, w_in_odd, b_forget, w_out_odd, ffn_w_gate, ffn_w_up, ffn_conv_w, ffn_conv_b, ffn_w_down):
    depth = norm_gains.shape[0]
    bp, tp, d = x_prompt.shape
    bs, ts, _ = x_sample.shape
    n_pages = page_table.shape[1]
    n_pool = cache_a_k.shape[1]
    dff = ffn_w_gate.shape[2]
    assert ts == 8 and cache_a_k.shape[2] == PAGE
    ns = bs * ts
    tm_p, tm_s = _tok_tile(tp, 512), _tok_tile(ns, 512)
    tq = _tok_tile(tp, 256)
    tf = dff // 2 if dff % (2 * LANES) == 0 else dff
    n_pp = 16 if n_pages % 16 == 0 else n_pages
    assert dff % tf == 0

    pos_p = jnp.arange(tp, dtype=jnp.int32)
    pos_s = n_pages * PAGE + jnp.arange(ts, dtype=jnp.int32)
    tabs_p = (_rope_tables(pos_p, A_ROT), _rope_tables(pos_p, QK_ROPE))
    tabs_s = tuple(tuple(jnp.tile(t, (tm_s // ts, 1)) for t in tabs)
                   for tabs in (_rope_tables(pos_s, A_ROT), _rope_tables(pos_s, QK_ROPE)))

    akT = jnp.transpose(cache_a_k, (0, 1, 3, 4, 5, 2)).reshape(-1, n_pool, 512, PAGE)
    av4 = cache_a_v.reshape(-1, n_pool, PAGE * A_HEADS, A_DV)
    bcT = jnp.swapaxes(cache_b_ckv, 2, 3)
    ckT = jnp.transpose(cache_c_k, (0, 1, 3, 4, 2)).reshape(-1, n_pool, 512, PAGE)
    cvT = jnp.transpose(cache_c_v, (0, 1, 3, 4, 2)).reshape(-1, n_pool, 512, PAGE)
    cfT = jnp.swapaxes(cache_c_logf, 2, 3)

    tri_ge = jnp.tril(jnp.ones((PAGE, PAGE), F32)).astype(BF16)
    tri_le = jnp.triu(jnp.ones((PAGE, PAGE), F32)).astype(BF16)

    hp = x_prompt
    hs = x_sample.reshape(1, ns, d)
    outs_p = {k: [] for k in ("ak", "av", "ckv", "ck", "cv", "cf", "conv")}
    outs_s = {k: [] for k in outs_p}

    for li in range(depth):
        gn = norm_gains[li]
        g0, g1, g2, g3 = (gn[k].reshape(1, d) for k in range(4))
        if li % 2 == 0:
            e = li // 2
            lam_init = _lambda_init(li)
            win = _pad_axis(w_in_even[e], 1, 2304).astype(BF16)
            wq_n = w_q_up[e][:, :, :QK_NOPE].reshape(Q_LORA, B_HEADS * QK_NOPE)
            wq_r = _pad_axis(w_q_up[e][:, :, QK_NOPE:], 2, LANES).reshape(Q_LORA, B_HEADS * LANES)
            wq = jnp.concatenate([wq_n, wq_r], axis=1).astype(BF16)
            wuk = jnp.transpose(w_kv_uk[e], (1, 2, 0)).astype(BF16)
            wuv = jnp.transpose(w_kv_uv[e], (1, 0, 2)).astype(BF16)
            wout = w_out_even[e].astype(BF16)
            gq, gkv = g_q_lat[e].reshape(1, -1), g_kv_lat[e].reshape(1, -1)
            lam, gsub = diff_lambda[e], g_diff_subln[e].reshape(1, -1)

            def even_in(h, tabs, tm):
                return _even_in(h, g0, win, gq, gkv, wq, wuk, tabs[0], tabs[1], tm)

            qa, ka, kab, va, vab, qlat, qrp, ckv, ckvb = even_in(hp, tabs_p, tm_p)
            oa = _prompt_attn(functools.partial(_prompt_a_kernel, lam_init), [qa, kab, vab, lam, gsub],
                              [_q_tile_spec(qa, tq), _seq_spec(kab), _seq_spec(vab), _full_spec(lam.shape),
                               _full_spec(gsub.shape)],
                              bp, tp, 512, 512, [(A_DV, 2 * tq)] * A_HEADS, tq, "prompt_attn_a")
            olat = _prompt_attn(_prompt_b_kernel, [qlat, qrp, ckvb],
                                [_q_tile_spec(qlat, tq), _q_tile_spec(qrp, tq), _seq_spec(ckvb)],
                                bp, tp, 1024, KV_LORA, [(KV_LORA, B_HEADS * tq)], tq, "prompt_attn_b")
            mix_p = (oa, olat)
            outs_p["ak"].append(ka.reshape(bp, tp, 2, A_HEADS, A_DH))
            outs_p["av"].append(va.reshape(bp, tp, A_HEADS, A_DV))
            outs_p["ckv"].append(ckv)

            qa, ka, kab, va, vab, qlat, qrp, ckv, ckvb = even_in(hs, tabs_s, tm_s)
            sel = (jnp.arange(8)[:, None] == jnp.arange(8).reshape(2, 4).T.reshape(8)[None, :])
            qx = jnp.transpose(qa.reshape(bs, ts, 2, A_HEADS, A_DH), (0, 3, 2, 1, 4)).reshape(bs, 8, ts, A_DH)
            q_bd = jnp.where(sel.T[None, :, None, :, None], qx[:, :, :, None, :], jnp.zeros((), BF16))
            q_bd = q_bd.reshape(bs, 8 * ts, 512)
            ka3, va3, ckv3 = ka.reshape(bs, ts, 512), va.reshape(bs, ts, 512), ckv.reshape(bs, ts, 320)
            kn = _feature_major_page(ka3)
            vn = _pad_axis(va3.reshape(bs, ts * A_HEADS, A_DV), 1, PAGE * A_HEADS)
            oa = _decode_attn(functools.partial(_decode_a_kernel, n_pp, lam_init), page_table, q_bd,
                              [akT[e:e + 1], av4[e:e + 1]], [kn, vn], [lam, gsub], 512, 64, A_DV, n_pp, False, [],
                              "decode_attn_a")
            q_abs = jnp.concatenate([qlat.reshape(bs, ts, B_HEADS, KV_LORA),
                                     qrp.reshape(bs, ts, B_HEADS, LANES)[..., :QK_ROPE]], axis=-1)
            q_abs = jnp.swapaxes(q_abs, 1, 2).reshape(bs, B_HEADS * ts, KV_LORA + QK_ROPE)
            olat = _decode_attn(functools.partial(_decode_b_kernel, n_pp), page_table, q_abs,
                                [bcT[e:e + 1]], [_feature_major_page(ckv3)], [], 1024, 32, KV_LORA, n_pp, False, [],
                                "decode_attn_b")
            mix_s = (oa.reshape(1, ns, 512), olat.reshape(1, ns, 1024))
            outs_s["ak"].append(ka.reshape(bs, ts, 2, A_HEADS, A_DH))
            outs_s["av"].append(va.reshape(bs, ts, A_HEADS, A_DV))
            outs_s["ckv"].append(ckv3)

            hp = _mix_out(_even_out_kernel, hp, mix_p, [wuv, wout, g1], tm_p, "even_out_proj")
            hs = _mix_out(_even_out_kernel, hs, mix_s, [wuv, wout, g1], tm_s, "even_out_proj")
        else:
            o = li // 2
            w = w_in_odd[o]
            nq, nk = C_HEADS * C_DH, C_KV_HEADS * C_DH
            wq_perm = w[:, :nq].reshape(d, C_HEADS, C_DH)[:, jnp.array(_HEAD_OF_SLOT), :].reshape(d, nq)
            win = jnp.concatenate([wq_perm, w[:, nq:nq + 2 * nk]], axis=1).astype(BF16)
            wft = jnp.transpose(w[:, nq + 2 * nk:]).astype(BF16)
            bf = b_forget[o].reshape(C_HEADS, 1)
            wout = w_out_odd[o].reshape(C_HEADS, C_DH, d)[jnp.array(_HEAD_OF_SLOT)].reshape(nq, d).astype(BF16)

            q, k, kb, v, vb, lft, ct, cn = _odd_in(hp, g0, win, wft, bf, tm_p)
            op = _prompt_attn(_prompt_c_kernel, [q, ct, kb, vb, cn],
                              [_q_tile_spec(q, tq), pl.BlockSpec((1, C_HEADS, tq), lambda b, i: (b, 0, i)),
                               _seq_spec(kb), _seq_spec(vb), _seq_spec(cn)],
                              bp, tp, 1024, 512, [(C_DH, 2 * tq)] * 2, tq, "prompt_attn_c")
            outs_p["ck"].append(k.reshape(bp, tp, C_KV_HEADS, C_DH))
            outs_p["cv"].append(v.reshape(bp, tp, C_KV_HEADS, C_DH))
            outs_p["cf"].append(jnp.swapaxes(lft, 1, 2))

            q, k, kb, v, vb, lft, ct, cn = _odd_in(hs, g0, win, wft, bf, tm_s)
            qh = q.reshape(bs, ts, C_HEADS, C_DH)[:, :, jnp.array(_SLOT_OF_HEAD), :]
            qh = jnp.swapaxes(qh, 1, 2)
            selc = (jnp.arange(C_HEADS)[:, None] // 2 == jnp.arange(C_KV_HEADS)[None, :])
            q_bd = jnp.where(selc[None, :, None, :, None], qh[:, :, :, None, :], jnp.zeros((), BF16))
            q_bd = q_bd.reshape(bs, C_HEADS * ts, 512)
            k3, v3 = k.reshape(bs, ts, 512), v.reshape(bs, ts, 512)
            lf3 = jnp.swapaxes(lft.reshape(C_HEADS, bs, ts), 0, 1)
            os_ = _decode_attn(functools.partial(_decode_c_kernel, n_pp), page_table, q_bd,
                               [ckT[o:o + 1], cvT[o:o + 1], cfT[o:o + 1]],
                               [_feature_major_page(k3), _feature_major_page(v3), _pad_axis(lf3, 2, PAGE)],
                               [tri_ge, tri_le], 1024, 128, 512, n_pp, True,
                               [pltpu.VMEM((C_HEADS, 1), F32), pltpu.VMEM((C_HEADS * 8, 1), F32)],
                               "decode_attn_c")
            outs_s["ck"].append(k3.reshape(bs, ts, C_KV_HEADS, C_DH))
            outs_s["cv"].append(v3.reshape(bs, ts, C_KV_HEADS, C_DH))
            outs_s["cf"].append(jnp.swapaxes(lf3, 1, 2))

            hp = _mix_out(_odd_out_kernel, hp, (op,), [wout, g1], tm_p, "odd_out_proj")
            hs = _mix_out(_odd_out_kernel, hs, (os_.reshape(1, ns, nq),), [wout, g1], tm_s, "odd_out_proj")

        wg, wu, wd = ffn_w_gate[li].astype(BF16), ffn_w_up[li].astype(BF16), ffn_w_down[li].astype(BF16)
        cw, cb = ffn_conv_w[li], ffn_conv_b[li].reshape(1, dff)
        hp, tail = _ffn(hp, g2, wg, wu, cw, cb, wd, g3, None, tm_p, tf)
        outs_p["conv"].append(tail[:, 8 - (CONV_W - 1):])
        buf = state_conv[li]
        prev2 = _pad_axis(buf, 1, ts).reshape(1, ns, dff)
        prev1 = _pad_axis(buf[:, 1:], 1, ts).reshape(1, ns, dff)
        hs, gate = _ffn(hs, g2, wg, wu, cw, cb, wd, g3, (prev2, prev1), tm_s, tf)
        outs_s["conv"].append(gate.reshape(bs, ts, dff)[:, ts - (CONV_W - 1):])

    st = lambda xs: jnp.stack(xs)
    res = [hp, hs.reshape(bs, ts, d)]
    for key in ("ak", "av", "ckv", "ck", "cv", "cf", "conv"):
        res += [st(outs_p[key]), st(outs_s[key])]
    return tuple(res)
```

```python
import functools
import math

import jax
import jax.numpy as jnp
from jax import lax
from jax.experimental import pallas as pl
from jax.experimental.pallas import tpu as pltpu

F32 = jnp.float32
BF16 = jnp.bfloat16

EPS = 1e-6
ROPE_THETA = 500000.0
A_HEADS, A_DH, A_DV, A_ROT = 4, 64, 128, 16
B_HEADS, Q_LORA, KV_LORA, QK_NOPE, QK_ROPE, B_DV = 4, 384, 256, 128, 64, 128
C_HEADS, C_KV_HEADS, C_DH = 16, 8, 64
CONV_W = 3
PAGE = 128
LANES = 128
FLASH_COLS = 128
NEG = -1e30
VMEM_LIMIT_BYTES = 56 * 1024 * 1024

_NT = (((1,), (1,)), ((), ()))


def _cp(sem):
    return pltpu.CompilerParams(dimension_semantics=sem, vmem_limit_bytes=VMEM_LIMIT_BYTES)


def _rms(x, g):
    return x * lax.rsqrt(jnp.mean(x * x, axis=-1, keepdims=True) + EPS) * g


def _dot(a, b):
    return jnp.dot(a, b, preferred_element_type=F32)


def _dot_nt(a, b):
    return lax.dot_general(a, b, _NT, preferred_element_type=F32)


def _rope_chunk(x, c, sl, sr, shift):
    return x * c + pltpu.roll(x, LANES - shift, 1) * sl + pltpu.roll(x, shift, 1) * sr


def _rope_tables(pos, rot):
    half = rot // 2
    inv = jnp.power(jnp.float32(ROPE_THETA), -jnp.arange(half, dtype=F32) / half)
    ang = pos.astype(F32)[:, None] * inv[None, :]
    cos, sin = jnp.cos(ang), jnp.sin(ang)
    t = pos.shape[0]
    one = jnp.ones((t, 64 - rot), F32)
    z_rest = jnp.zeros((t, 64 - rot), F32)
    z_half = jnp.zeros((t, half), F32)
    c = jnp.concatenate([cos, cos, one], axis=1)
    sl = jnp.concatenate([-sin, z_half, z_rest], axis=1)
    sr = jnp.concatenate([z_half, sin, z_rest], axis=1)
    return tuple(jnp.tile(a, (1, 2)) for a in (c, sl, sr))


def _even_in_kernel(h_ref, g0_ref, win_ref, gq_ref, gkv_ref, wq_ref, wuk_ref,
                    ca_ref, sla_ref, sra_ref, cb_ref, slb_ref, srb_ref,
                    qa_ref, ka_ref, kab_ref, va_ref, vab_ref, qlat_ref, qrp_ref, ckv_ref, ckvb_ref):
    hn = _rms(h_ref[0], g0_ref[...]).astype(BF16)
    z = _dot(hn, win_ref[...])
    ca, sla, sra = ca_ref[...], sla_ref[...], sra_ref[...]
    cb, slb, srb = cb_ref[...], slb_ref[...], srb_ref[...]
    for j in range(4):
        sl = slice(LANES * j, LANES * (j + 1))
        qa_ref[0, :, sl] = (_rope_chunk(z[:, sl], ca, sla, sra, A_ROT // 2) * (A_DH ** -0.5)).astype(BF16)
        kr = _rope_chunk(z[:, 512 + LANES * j:512 + LANES * (j + 1)], ca, sla, sra, A_ROT // 2)
        ka_ref[0, :, sl] = kr
        kab_ref[0, :, sl] = kr.astype(BF16)
    va = z[:, 1024:1536]
    va_ref[0] = va
    vab_ref[0] = va.astype(BF16)
    cqn = _rms(z[:, 1536:1920], gq_ref[...]).astype(BF16)
    q2 = _dot(cqn, wq_ref[...])
    for h in range(B_HEADS):
        qn = q2[:, LANES * h:LANES * (h + 1)].astype(BF16)
        qlat_ref[0, :, KV_LORA * h:KV_LORA * (h + 1)] = _dot(qn, wuk_ref[h]).astype(BF16)
        qr = _rope_chunk(q2[:, 512 + LANES * h:512 + LANES * (h + 1)], cb, slb, srb, QK_ROPE // 2)
        qrp_ref[0, :, LANES * h:LANES * (h + 1)] = qr.astype(BF16)
    cn = _rms(z[:, 1920:2176], gkv_ref[...])
    krr = _rope_chunk(z[:, 2176:2304], cb, slb, srb, QK_ROPE // 2)
    ckv_ref[0, :, 0:KV_LORA] = cn
    ckv_ref[0, :, KV_LORA:KV_LORA + QK_ROPE] = krr[:, :QK_ROPE]
    ckvb_ref[0, :, 0:KV_LORA] = cn.astype(BF16)
    ckvb_ref[0, :, KV_LORA:KV_LORA + LANES] = krr.astype(BF16)


def _tab_spec(tm, n_tab_tiles):
    return pl.BlockSpec((tm, LANES), lambda b, i: (i % n_tab_tiles, 0))


def _full_spec(shape):
    nd = len(shape)
    return pl.BlockSpec(shape, lambda b, i: (0,) * nd)


def _even_in(h, g0, win, gq, gkv, wq, wuk, tabs_a, tabs_b, tm):
    bk, tk, d = h.shape
    n_tab = tabs_a[0].shape[0] // tm
    tok = lambda w: pl.BlockSpec((1, tm, w), lambda b, i: (b, i, 0))
    out_shapes = [
        jax.ShapeDtypeStruct((bk, tk, 512), BF16),
        jax.ShapeDtypeStruct((bk, tk, 512), F32),
        jax.ShapeDtypeStruct((bk, tk, 512), BF16),
        jax.ShapeDtypeStruct((bk, tk, 512), F32),
        jax.ShapeDtypeStruct((bk, tk, 512), BF16),
        jax.ShapeDtypeStruct((bk, tk, 1024), BF16),
        jax.ShapeDtypeStruct((bk, tk, 512), BF16),
        jax.ShapeDtypeStruct((bk, tk, 320), F32),
        jax.ShapeDtypeStruct((bk, tk, 384), BF16),
    ]
    return pl.pallas_call(
        _even_in_kernel,
        grid=(bk, tk // tm),
        in_specs=[tok(d), _full_spec(g0.shape), _full_spec(win.shape), _full_spec(gq.shape),
                  _full_spec(gkv.shape), _full_spec(wq.shape), _full_spec(wuk.shape)]
                 + [_tab_spec(tm, n_tab)] * 6,
        out_specs=[tok(s.shape[-1]) for s in out_shapes],
        out_shape=out_shapes,
        compiler_params=_cp(("parallel", "parallel")),
        name="even_in_proj",
    )(h, g0, win, gq, gkv, wq, wuk, *tabs_a, *tabs_b)


def _split3(x):
    hi = x.astype(BF16)
    r = x - hi.astype(F32)
    mid = r.astype(BF16)
    lo = (r - mid.astype(F32)).astype(BF16)
    return hi, mid, lo


def _tri_dot(x, tri):
    n = x.shape[0]
    hi, mid, lo = _split3(x)
    y = _dot(jnp.concatenate([hi, mid, lo], axis=0), tri)
    return y[0:n] + y[n:2 * n] + y[2 * n:3 * n]


def _log_sigmoid(x):
    return -(jnp.maximum(-x, 0.0) + jnp.log1p(jnp.exp(-jnp.abs(x))))


def _odd_in_kernel(h_ref, g0_ref, win_ref, wft_ref, bf_ref, tri_ref,
                   q_ref, k_ref, kb_ref, v_ref, vb_ref, lft_ref, ct_ref, cn_ref, carry_ref):
    i = pl.program_id(1)
    hn = _rms(h_ref[0], g0_ref[...]).astype(BF16)
    z = _dot(hn, win_ref[...])
    q_ref[0] = (z[:, 0:1024] * (C_DH ** -0.5)).astype(BF16)
    k = z[:, 1024:1536]
    k_ref[0] = k
    kb_ref[0] = k.astype(BF16)
    v = z[:, 1536:2048]
    v_ref[0] = v
    vb_ref[0] = v.astype(BF16)
    ft = _dot_nt(wft_ref[...], hn)
    lft = _log_sigmoid(ft + bf_ref[...])
    lft_ref[0] = lft

    @pl.when(i == 0)
    def _():
        carry_ref[...] = jnp.zeros_like(carry_ref)

    c = _tri_dot(lft, tri_ref[...]) + carry_ref[...]
    carry_ref[...] = c[:, -1:]
    ct_ref[0] = c
    c128 = jnp.concatenate([c, jnp.zeros((LANES - C_HEADS, c.shape[1]), F32)], axis=0)
    cn_ref[0] = jnp.transpose(c128)[:, 0:C_HEADS]


def _odd_in(h, g0, win, wft, bf, tm):
    bk, tk, d = h.shape
    tri = jnp.triu(jnp.ones((tm, tm), F32)).astype(BF16)
    tok = lambda w: pl.BlockSpec((1, tm, w), lambda b, i: (b, i, 0))
    tokt = pl.BlockSpec((1, C_HEADS, tm), lambda b, i: (b, 0, i))
    out_shapes = [
        jax.ShapeDtypeStruct((bk, tk, 1024), BF16),
        jax.ShapeDtypeStruct((bk, tk, 512), F32),
        jax.ShapeDtypeStruct((bk, tk, 512), BF16),
        jax.ShapeDtypeStruct((bk, tk, 512), F32),
        jax.ShapeDtypeStruct((bk, tk, 512), BF16),
        jax.ShapeDtypeStruct((bk, C_HEADS, tk), F32),
        jax.ShapeDtypeStruct((bk, C_HEADS, tk), F32),
        jax.ShapeDtypeStruct((bk, tk, C_HEADS), F32),
    ]
    return pl.pallas_call(
        _odd_in_kernel,
        grid=(bk, tk // tm),
        in_specs=[tok(d), _full_spec(g0.shape), _full_spec(win.shape), _full_spec(wft.shape),
                  _full_spec(bf.shape), _full_spec(tri.shape)],
        out_specs=[tok(1024), tok(512), tok(512), tok(512), tok(512), tokt, tokt, tok(C_HEADS)],
        out_shape=out_shapes,
        scratch_shapes=[pltpu.VMEM((C_HEADS, 1), F32)],
        compiler_params=_cp(("parallel", "arbitrary")),
        name="odd_in_proj",
    )(h, g0, win, wft, bf, tri)


def _even_out_kernel(h_ref, oa_ref, olat_ref, wuv_ref, wout_ref, g1_ref, out_ref):
    parts = [oa_ref[0]]
    for hd in range(B_HEADS):
        parts.append(_dot(olat_ref[0, :, KV_LORA * hd:KV_LORA * (hd + 1)], wuv_ref[hd]).astype(BF16))
    y = _dot(jnp.concatenate(parts, axis=1), wout_ref[...])
    out_ref[0] = h_ref[0] + _rms(y, g1_ref[...])


def _odd_out_kernel(h_ref, o_ref, wout_ref, g1_ref, out_ref):
    y = _dot(o_ref[0], wout_ref[...])
    out_ref[0] = h_ref[0] + _rms(y, g1_ref[...])


def _mix_out(kern, h, acts, consts, tm, name):
    bk, tk, d = h.shape
    tok = lambda w: pl.BlockSpec((1, tm, w), lambda b, i: (b, i, 0))
    return pl.pallas_call(
        kern,
        grid=(bk, tk // tm),
        in_specs=[tok(d)] + [tok(a.shape[-1]) for a in acts] + [_full_spec(c.shape) for c in consts],
        out_specs=tok(d),
        out_shape=jax.ShapeDtypeStruct(h.shape, F32),
        compiler_params=_cp(("parallel", "parallel")),
        name=name,
    )(h, *acts, *consts)


def _ffn_kernel(short_seq, *refs):
    if short_seq:
        (h_ref, g2_ref, wg_ref, wu_ref, cw_ref, cb_ref, wd_ref, g3_ref, pa_ref, pb_ref,
         out_ref, gate_ref, hn_ref, acc_ref) = refs
    else:
        (h_ref, g2_ref, wg_ref, wu_ref, cw_ref, cb_ref, wd_ref, g3_ref,
         out_ref, tail_ref, hn_ref, acc_ref, carry_ref) = refs
    i = pl.program_id(1)
    c = pl.program_id(2)

    @pl.when(c == 0)
    def _():
        hn_ref[...] = _rms(h_ref[0], g2_ref[...]).astype(BF16)
        acc_ref[...] = jnp.zeros_like(acc_ref)

    hn = hn_ref[...]
    g = _dot(hn, wg_ref[...])
    u = _dot(hn, wu_ref[...])
    tm = g.shape[0]
    row = lax.broadcasted_iota(jnp.int32, g.shape, 0)
    r1 = pltpu.roll(g, 1, 0)
    r2 = pltpu.roll(g, 2, 0)
    if short_seq:
        t = row % 8
        gm1 = jnp.where(t == 0, pb_ref[0], r1)
        gm2 = jnp.where(t < 2, pa_ref[0], r2)
        gate_ref[0] = g
    else:
        top = jnp.where(i == 0, 0.0, carry_ref[c])
        gm1 = jnp.where(row == 0, top[7:8], r1)
        gm2 = jnp.where(row == 0, top[6:7], jnp.where(row == 1, top[7:8], r2))
        carry_ref[c] = g[tm - 8:tm]
        tail_ref[0] = g[tm - 8:tm]
    cw = cw_ref[...]
    gc = cb_ref[...] + ((cw[0:1] * gm2 + cw[1:2] * gm1) + cw[2:3] * g)
    act = (jax.nn.silu(gc) * u).astype(BF16)
    acc_ref[...] += _dot(act, wd_ref[...])

    @pl.when(c == pl.num_programs(2) - 1)
    def _():
        out_ref[0] = h_ref[0] + _rms(acc_ref[...], g3_ref[...])


def _ffn(h, g2, wg, wu, cw, cb, wd, g3, prev, tm, tf):
    bk, tk, d = h.shape
    dff = wg.shape[1]
    nc = dff // tf
    short_seq = prev is not None
    tok = pl.BlockSpec((1, tm, d), lambda b, i, c: (b, i, 0))
    const = lambda shape: pl.BlockSpec(shape, lambda b, i, c: (0,) * len(shape))
    in_specs = [tok, const(g2.shape),
                pl.BlockSpec((d, tf), lambda b, i, c: (0, c)),
                pl.BlockSpec((d, tf), lambda b, i, c: (0, c)),
                pl.BlockSpec((CONV_W, tf), lambda b, i, c: (0, c)),
                pl.BlockSpec((1, tf), lambda b, i, c: (0, c)),
                pl.BlockSpec((tf, d), lambda b, i, c: (c, 0)),
                const(g3.shape)]
    args = [h, g2, wg, wu, cw, cb, wd, g3]
    scratch = [pltpu.VMEM((tm, d), BF16), pltpu.VMEM((tm, d), F32)]
    if short_seq:
        in_specs += [pl.BlockSpec((1, tm, tf), lambda b, i, c: (b, i, c))] * 2
        args += list(prev)
        out_specs = [tok, pl.BlockSpec((1, tm, tf), lambda b, i, c: (b, i, c))]
        out_shape = [jax.ShapeDtypeStruct(h.shape, F32), jax.ShapeDtypeStruct((bk, tk, dff), F32)]
    else:
        out_specs = [tok, pl.BlockSpec((1, 8, tf), lambda b, i, c: (b, 0, c))]
        out_shape = [jax.ShapeDtypeStruct(h.shape, F32), jax.ShapeDtypeStruct((bk, 8, dff), F32)]
        scratch.append(pltpu.VMEM((nc, 8, tf), F32))
    return pl.pallas_call(
        functools.partial(_ffn_kernel, short_seq),
        grid=(bk, tk // tm, nc),
        in_specs=in_specs,
        out_specs=out_specs,
        out_shape=out_shape,
        scratch_shapes=scratch,
        compiler_params=_cp(("parallel", "arbitrary", "arbitrary")),
        name="conv_ffn",
    )(*args)


def _flash_cols(i, tq, chains):
    subs = []
    for c in chains:
        c["acc"][...] = jnp.zeros_like(c["acc"])
        for c0 in range(0, c["acc"].shape[1], c["cw"]):
            subs.append((c, slice(c0, c0 + c["cw"])))

    def step(j, stats, masked):
        kv = pl.ds(pl.multiple_of(j * tq, tq), tq)
        new_stats = []
        for (c, cols), (m_prev, l_prev) in zip(subs, stats):
            s = c["qk"](kv, cols)
            if c["scale"] is not None:
                s = s * c["scale"]
            if c["bias"] is not None:
                s = s + c["bias"](kv, cols)
            if masked:
                key = lax.broadcasted_iota(jnp.int32, s.shape, 0)
                qpos = (lax.broadcasted_iota(jnp.int32, s.shape, 1) + cols.start) % tq
                s = jnp.where(key <= qpos, s, NEG)
            m_new = jnp.maximum(m_prev, jnp.max(s, axis=0, keepdims=True))
            alpha = jnp.exp(m_prev - m_new)
            p = jnp.exp(s - m_new)
            c["acc"][:, cols] = alpha * c["acc"][:, cols] + _dot(c["vt"](kv), p.astype(BF16))
            new_stats.append((m_new, alpha * l_prev + jnp.sum(p, axis=0, keepdims=True)))
        return tuple(new_stats)

    init = tuple((jnp.full((1, c["cw"]), NEG, F32), jnp.zeros((1, c["cw"]), F32)) for c, _ in subs)
    stats = lax.fori_loop(0, i, lambda j, st: step(j, st, False), init)
    stats = step(i, stats, True)
    outs, k = [], 0
    for c in chains:
        n = c["acc"].shape[1] // c["cw"]
        l = jnp.concatenate([stats[k + t][1] for t in range(n)], axis=1)
        outs.append(c["acc"][...] / l)
        k += n
    return outs


def _transpose_bf16(x):
    return jnp.transpose(x.astype(F32)).astype(BF16)


def _fill_transposed(i, tq, src_fn, dst_ref):
    n_tiles = dst_ref.shape[1] // tq

    @pl.when(i == 0)
    def _():
        def body(c, carry):
            rows = pl.ds(pl.multiple_of(c * tq, tq), tq)
            dst_ref[:, rows] = _transpose_bf16(src_fn(rows))
            return carry

        lax.fori_loop(0, n_tiles, body, 0)


def _chain_scratch(dv, r):
    return [pltpu.VMEM((dv, r), F32)]


def _lambda_value(lam_ref, lam_init):
    lam = lam_ref[...]
    s01 = jnp.sum(lam[0:1] * lam[1:2], axis=-1, keepdims=True)
    s23 = jnp.sum(lam[2:3] * lam[3:4], axis=-1, keepdims=True)
    return jnp.exp(s01) - jnp.exp(s23) + lam_init


def _half_rows(blk, half):
    z = jnp.zeros_like(blk)
    return jnp.concatenate([blk, z] if half == 0 else [z, blk], axis=0)


def _prompt_a_kernel(lam_init, q_ref, k_ref, v_ref, lam_ref, gsub_ref, out_ref, vt_ref, *st):
    i = pl.program_id(1)
    tq = q_ref.shape[1]
    _fill_transposed(i, tq, lambda rows: v_ref[0, rows, :], vt_ref)
    qt = _transpose_bf16(q_ref[0])
    lam_val = _lambda_value(lam_ref, lam_init)
    for pair in range(A_HEADS // 2):
        chains = []
        for hd in (2 * pair, 2 * pair + 1):
            slots = [s * A_HEADS + hd for s in range(2)]
            qts = [_half_rows(qt[64 * e:64 * (e + 1)], e % 2) for e in slots]
            ksl = [slice(LANES * (e // 2), LANES * (e // 2 + 1)) for e in slots]

            def qk(kv, cols, qts=qts, ksl=ksl):
                s, lo = divmod(cols.start, tq)
                return _dot(k_ref[0, kv, ksl[s]], qts[s][:, lo:lo + cols.stop - cols.start])

            def vt(kv, hd=hd):
                return vt_ref[A_DV * hd:A_DV * (hd + 1), kv]

            chains.append(dict(qk=qk, vt=vt, scale=None, bias=None, acc=st[hd], cw=FLASH_COLS))
        for hd, ot in zip((2 * pair, 2 * pair + 1), _flash_cols(i, tq, chains)):
            d = jnp.transpose(ot[:, 0:tq] - lam_val * ot[:, tq:2 * tq])
            out_ref[0, :, A_DV * hd:A_DV * (hd + 1)] = (_rms(d, gsub_ref[...]) * (1.0 - lam_init)).astype(BF16)


def _prompt_b_kernel(q1_ref, q2_ref, c_ref, out_ref, vt_ref, acc_ref):
    i = pl.program_id(1)
    tq = q1_ref.shape[1]
    _fill_transposed(i, tq, lambda rows: c_ref[0, rows, 0:KV_LORA], vt_ref)
    q1t = _transpose_bf16(q1_ref[0])
    q2t = _transpose_bf16(q2_ref[0])
    q1s = jnp.concatenate([q1t[KV_LORA * hd:KV_LORA * (hd + 1)] for hd in range(B_HEADS)], axis=1)
    q2s = jnp.concatenate([q2t[LANES * hd:LANES * (hd + 1)] for hd in range(B_HEADS)], axis=1)

    def qk(kv, cols):
        return (_dot(c_ref[0, kv, 0:KV_LORA], q1s[:, cols])
                + _dot(c_ref[0, kv, KV_LORA:KV_LORA + LANES], q2s[:, cols]))

    chain = dict(qk=qk, vt=lambda kv: vt_ref[:, kv], scale=(QK_NOPE + QK_ROPE) ** -0.5, bias=None, acc=acc_ref,
                 cw=B_HEADS * tq)
    (ot,) = _flash_cols(i, tq, [chain])
    for hd in range(B_HEADS):
        out_ref[0, :, KV_LORA * hd:KV_LORA * (hd + 1)] = jnp.transpose(ot[:, tq * hd:tq * (hd + 1)]).astype(BF16)


def _prompt_c_kernel(q_ref, ct_ref, k_ref, v_ref, cn_ref, out_ref, vt_ref, *st):
    i = pl.program_id(1)
    tq = q_ref.shape[1]
    _fill_transposed(i, tq, lambda rows: v_ref[0, rows, :], vt_ref)
    qt = _transpose_bf16(q_ref[0])
    for pp in range(C_KV_HEADS // 2):
        chains = []
        for half in range(2):
            heads = [4 * pp + 2 * half + r for r in range(2)]
            blk = jnp.concatenate(
                [qt[LANES * (2 * pp + r) + 64 * half:LANES * (2 * pp + r) + 64 * (half + 1)] for r in range(2)],
                axis=1)
            qs = _half_rows(blk, half)
            cq = [ct_ref[0, hd:hd + 1, :] for hd in heads]

            def bias(kv, cols, heads=heads, cq=cq):
                return jnp.concatenate([cq[r] - cn_ref[0, kv, hd:hd + 1] for r, hd in enumerate(heads)], axis=1)

            def qk(kv, cols, qs=qs):
                return _dot(k_ref[0, kv, LANES * pp:LANES * (pp + 1)], qs[:, cols])

            def vt(kv, g=2 * pp + half):
                return vt_ref[C_DH * g:C_DH * (g + 1), kv]

            chains.append(dict(qk=qk, vt=vt, scale=None, bias=bias, acc=st[half], cw=2 * tq))
        ot = jnp.concatenate(_flash_cols(i, tq, chains), axis=0)
        for r in range(2):
            out_ref[0, :, LANES * (2 * pp + r):LANES * (2 * pp + r + 1)] = jnp.transpose(
                ot[:, tq * r:tq * (r + 1)]).astype(BF16)


def _prompt_attn(kern, args, in_specs, bk, t, out_w, vt_rows, chain_shapes, tq, name):
    scratch = [pltpu.VMEM((vt_rows, t), BF16)]
    for dv, r in chain_shapes:
        scratch += _chain_scratch(dv, r)
    return pl.pallas_call(
        kern,
        grid=(bk, t // tq),
        in_specs=in_specs,
        out_specs=pl.BlockSpec((1, tq, out_w), lambda b, i: (b, i, 0)),
        out_shape=jax.ShapeDtypeStruct((bk, t, out_w), BF16),
        scratch_shapes=scratch,
        compiler_params=_cp(("parallel", "arbitrary")),
        name=name,
    )(*args)


def _q_tile_spec(a, tq):
    return pl.BlockSpec((1, tq, a.shape[-1]), lambda b, i: (b, i, 0))


def _seq_spec(a):
    return pl.BlockSpec((1,) + a.shape[1:], lambda b, i: (b, 0, 0))


def _online_update(s_list, pv_fn, m_ref, l_ref, acc_ref):
    m_prev = m_ref[...]
    m_new = m_prev
    for s in s_list:
        m_new = jnp.maximum(m_new, jnp.max(s, axis=-1, keepdims=True))
    alpha = jnp.exp(m_prev - m_new)
    ps = [jnp.exp(s - m_new) for s in s_list]
    l_new = alpha * l_ref[...]
    for p in ps:
        l_new = l_new + jnp.sum(p, axis=-1, keepdims=True)
    l_ref[...] = l_new
    acc_ref[...] = alpha * acc_ref[...] + pv_fn([p.astype(BF16) for p in ps])
    m_ref[...] = m_new


def _init_softmax(m_ref, l_ref, acc_ref):
    m_ref[...] = jnp.full_like(m_ref, NEG)
    l_ref[...] = jnp.zeros_like(l_ref)
    acc_ref[...] = jnp.zeros_like(acc_ref)


def _new_token_mask(s):
    row = lax.broadcasted_iota(jnp.int32, s.shape, 0) % 8
    col = lax.broadcasted_iota(jnp.int32, s.shape, 1)
    return jnp.where(col <= row, s, NEG)


def _decode_ab_kernel(n_pp, lam_init, pt_ref, qa_ref, qb_ref, *refs):
    ak_refs, av_refs, bc_refs = refs[:n_pp], refs[n_pp:2 * n_pp], refs[2 * n_pp:3 * n_pp]
    (kn_ref, vn_ref, cn_ref, lam_ref, gsub_ref, oa_ref, ob_ref,
     ma_ref, la_ref, acca_ref, mb_ref, lb_ref, accb_ref) = refs[3 * n_pp:]
    j = pl.program_id(1)
    qa = qa_ref[0]
    qb = qb_ref[0]
    scale_b = (QK_NOPE + QK_ROPE) ** -0.5

    def update_a(k_pages, v_of, mask):
        s_list = [_dot(qa, k) for k in k_pages]
        if mask:
            s_list = [_new_token_mask(s) for s in s_list]

        def pv(ps):
            outs = []
            for hd in range(A_HEADS):
                o = None
                for p, pr in enumerate(ps):
                    t = _dot(pr[16 * hd:16 * (hd + 1)], v_of(p, hd))
                    o = t if o is None else o + t
                outs.append(o)
            return jnp.concatenate(outs, axis=0)

        _online_update(s_list, pv, ma_ref, la_ref, acca_ref)

    def update_b(pages, mask):
        s_list = [_dot(qb, c) * scale_b for c in pages]
        if mask:
            s_list = [_new_token_mask(s) for s in s_list]

        def pv(ps):
            o = None
            for pr, c in zip(ps, pages):
                t = _dot_nt(pr, c[0:KV_LORA])
                o = t if o is None else o + t
            return o

        _online_update(s_list, pv, mb_ref, lb_ref, accb_ref)

    @pl.when(j == 0)
    def _():
        _init_softmax(ma_ref, la_ref, acca_ref)
        _init_softmax(mb_ref, lb_ref, accb_ref)

    update_a([ak_refs[p][0, 0].astype(BF16) for p in range(n_pp)],
             lambda p, hd: av_refs[p][0, 0, pl.ds(hd, PAGE, stride=A_HEADS), :].astype(BF16), False)
    update_b([bc_refs[p][0, 0].astype(BF16) for p in range(n_pp)], False)

    @pl.when(j == pl.num_programs(1) - 1)
    def _():
        update_a([kn_ref[0].astype(BF16)],
                 lambda p, hd: vn_ref[0, pl.ds(hd, PAGE, stride=A_HEADS), :].astype(BF16), True)
        update_b([cn_ref[0].astype(BF16)], True)
        o = acca_ref[...] / la_ref[...]
        lam_val = _lambda_value(lam_ref, lam_init)
        for hd in range(A_HEADS):
            d = o[16 * hd:16 * hd + 8] - lam_val * o[16 * hd + 8:16 * hd + 16]
            oa_ref[0, :, A_DV * hd:A_DV * (hd + 1)] = (
                _rms(d, gsub_ref[...]) * (1.0 - lam_init)).astype(BF16)
        o = accb_ref[...] / lb_ref[...]
        for hd in range(B_HEADS):
            ob_ref[0, :, KV_LORA * hd:KV_LORA * (hd + 1)] = o[8 * hd:8 * (hd + 1)].astype(BF16)


def _expand_heads(x):
    return jnp.broadcast_to(x[:, None, :], (C_HEADS, 8, x.shape[-1])).reshape(C_HEADS * 8, x.shape[-1])


def _decode_c_kernel(n_pp, pt_ref, q_ref, *refs):
    k_refs, v_refs, lf_refs = refs[:n_pp], refs[n_pp:2 * n_pp], refs[2 * n_pp:3 * n_pp]
    (kn_ref, vn_ref, lfn_ref, tri_ge_ref, tri_le_ref, out_ref,
     m_ref, l_ref, acc_ref, base_ref, bq_ref) = refs[3 * n_pp:]
    j = pl.program_id(1)
    last = pl.num_programs(1) - 1
    q = q_ref[0]

    def scores(k_page):
        return _dot(q, k_page)

    def pv_of(pages):
        def pv(ps):
            o = None
            for pr, v in zip(ps, pages):
                t = _dot_nt(pr, v)
                o = t if o is None else o + t
            return o
        return pv

    @pl.when(j == 0)
    def _():
        _init_softmax(m_ref, l_ref, acc_ref)
        base_ref[...] = jnp.zeros_like(base_ref)
        cnl = _tri_dot(lfn_ref[0], tri_le_ref[...])
        rep = _expand_heads(cnl)
        row = lax.broadcasted_iota(jnp.int32, rep.shape, 0) % 8
        col = lax.broadcasted_iota(jnp.int32, rep.shape, 1)
        bq = jnp.sum(jnp.where(col == row, rep, 0.0), axis=-1, keepdims=True)
        bq_ref[...] = bq
        s_new = _new_token_mask(scores(kn_ref[0].astype(BF16)) + (bq - rep))
        _online_update([s_new], pv_of([vn_ref[0].astype(BF16)]), m_ref, l_ref, acc_ref)

    lf = jnp.concatenate([lf_refs[p][0, 0] for p in range(n_pp)], axis=0)
    incl = _tri_dot(lf, tri_ge_ref[...])
    excl = incl - lf
    bq = bq_ref[...]
    base = base_ref[...]
    s_list = []
    for p in range(n_pp):
        sl = slice(C_HEADS * p, C_HEADS * (p + 1))
        bias = _expand_heads(base + excl[sl])
        s_list.append(scores(k_refs[p][0, 0].astype(BF16)) + (bq + bias))
        base = base + incl[sl, 0:1]
    base_ref[...] = base
    _online_update(s_list, pv_of([v_refs[p][0, 0].astype(BF16) for p in range(n_pp)]), m_ref, l_ref, acc_ref)

    @pl.when(j == last)
    def _():
        o = acc_ref[...] / l_ref[...]
        lane = lax.broadcasted_iota(jnp.int32, (8, LANES), 1)
        for pp in range(C_KV_HEADS // 2):
            csl = slice(LANES * pp, LANES * (pp + 1))
            for r in range(2):
                h0, h1 = 4 * pp + r, 4 * pp + 2 + r
                chunk = jnp.where(lane < 64, o[8 * h0:8 * (h0 + 1), csl], o[8 * h1:8 * (h1 + 1), csl])
                out_ref[0, :, LANES * (2 * pp + r):LANES * (2 * pp + r + 1)] = chunk.astype(BF16)


def _softmax_scratch(rows, dv):
    return [pltpu.VMEM((rows, 1), F32), pltpu.VMEM((rows, 1), F32), pltpu.VMEM((rows, dv), F32)]


def _decode_attn(kern, page_table, qs, caches, news, consts, out_ws, scratch, n_pp, reverse, name):
    nb, n_pages = page_table.shape
    n_steps = n_pages // n_pp

    def page_spec(c, p):
        def imap(b, j, pt):
            idx = j * n_pp + p
            if reverse:
                idx = n_pages - 1 - idx
            return (0, pt[b, idx], 0, 0)
        return pl.BlockSpec((1, 1) + c.shape[2:], imap)

    in_specs, args = [], []
    for q in qs:
        in_specs.append(pl.BlockSpec((1,) + q.shape[1:], lambda b, j, pt: (b, 0, 0)))
        args.append(q)
    for c in caches:
        for p in range(n_pp):
            in_specs.append(page_spec(c, p))
            args.append(c)
    for a in news:
        in_specs.append(pl.BlockSpec((1,) + a.shape[1:], lambda b, j, pt: (b, 0, 0)))
        args.append(a)
    for a in consts:
        in_specs.append(pl.BlockSpec(a.shape, lambda b, j, pt, nd=a.ndim: (0,) * nd))
        args.append(a)
    grid_spec = pltpu.PrefetchScalarGridSpec(
        num_scalar_prefetch=1,
        grid=(nb, n_steps),
        in_specs=in_specs,
        out_specs=[pl.BlockSpec((1, 8, w), lambda b, j, pt: (b, 0, 0)) for w in out_ws],
        scratch_shapes=scratch,
    )
    return pl.pallas_call(
        kern,
        grid_spec=grid_spec,
        out_shape=[jax.ShapeDtypeStruct((nb, 8, w), BF16) for w in out_ws],
        compiler_params=_cp(("parallel", "arbitrary")),
        name=name,
    )(page_table, *args)


_HEAD_OF_SLOT = [4 * (c // 2) + 2 * half + (c % 2) for c in range(C_HEADS // 2) for half in range(2)]
_SLOT_OF_HEAD = [_HEAD_OF_SLOT.index(h) for h in range(C_HEADS)]


def _lambda_init(layer):
    return 0.8 - 0.6 * math.exp(-0.3 * layer)


def _pad_axis(x, axis, size):
    pad = [(0, 0)] * x.ndim
    pad[axis] = (0, size - x.shape[axis])
    return jnp.pad(x, pad)


def _tok_tile(n, pref):
    tm = min(n, pref)
    assert n % tm == 0 and tm % 8 == 0
    return tm


def _feature_major_page(x):
    return _pad_axis(jnp.swapaxes(x, 1, 2), 2, PAGE)


def kernel(x_prompt, x_sample, cache_a_k, cache_a_v, cache_b_ckv, cache_c_k, cache_c_v, cache_c_logf, state_conv, page_table, norm_gains, w_in_even, w_q_up, w_kv_uk, w_kv_uv, g_q_lat, g_kv_lat, diff_lambda, g_diff_subln, w_out_even, w_in_odd, b_forget, w_out_odd, ffn_w_gate, ffn_w_up, ffn_conv_w, ffn_conv_b, ffn_w_down):
    depth = norm_gains.shape[0]
    bp, tp, d = x_prompt.shape
    bs, ts, _ = x_sample.shape
    n_pages = page_table.shape[1]
    n_pool = cache_a_k.shape[1]
    dff = ffn_w_gate.shape[2]
    assert ts == 8 and cache_a_k.shape[2] == PAGE
    ns = bs * ts
    tm_p, tm_s = _tok_tile(tp, 512), _tok_tile(ns, 512)
    tq = _tok_tile(tp, 256)
    tf = dff // 2 if dff % (2 * LANES) == 0 else dff
    n_pp = 16 if n_pages % 16 == 0 else n_pages
    assert dff % tf == 0

    pos_p = jnp.arange(tp, dtype=jnp.int32)
    pos_s = n_pages * PAGE + jnp.arange(ts, dtype=jnp.int32)
    tabs_p = (_rope_tables(pos_p, A_ROT), _rope_tables(pos_p, QK_ROPE))
    tabs_s = tuple(tuple(jnp.tile(t, (tm_s // ts, 1)) for t in tabs)
                   for tabs in (_rope_tables(pos_s, A_ROT), _rope_tables(pos_s, QK_ROPE)))

    akT = jnp.transpose(cache_a_k, (0, 1, 3, 4, 5, 2)).reshape(-1, n_pool, 512, PAGE)
    av4 = cache_a_v.reshape(-1, n_pool, PAGE * A_HEADS, A_DV)
    bcT = jnp.swapaxes(cache_b_ckv, 2, 3)
    ckT = jnp.transpose(cache_c_k, (0, 1, 3, 4, 2)).reshape(-1, n_pool, 512, PAGE)
    cvT = jnp.transpose(cache_c_v, (0, 1, 3, 4, 2)).reshape(-1, n_pool, 512, PAGE)
    cfT = jnp.swapaxes(cache_c_logf, 2, 3)

    tri_ge = jnp.tril(jnp.ones((PAGE, PAGE), F32)).astype(BF16)
    tri_le = jnp.triu(jnp.ones((PAGE, PAGE), F32)).astype(BF16)

    hp = x_prompt
    hs = x_sample.reshape(1, ns, d)
    outs_p = {k: [] for k in ("ak", "av", "ckv", "ck", "cv", "cf", "conv")}
    outs_s = {k: [] for k in outs_p}

    for li in range(depth):
        gn = norm_gains[li]
        g0, g1, g2, g3 = (gn[k].reshape(1, d) for k in range(4))
        if li % 2 == 0:
            e = li // 2
            lam_init = _lambda_init(li)
            win = _pad_axis(w_in_even[e], 1, 2304).astype(BF16)
            wq_n = w_q_up[e][:, :, :QK_NOPE].reshape(Q_LORA, B_HEADS * QK_NOPE)
            wq_r = _pad_axis(w_q_up[e][:, :, QK_NOPE:], 2, LANES).reshape(Q_LORA, B_HEADS * LANES)
            wq = jnp.concatenate([wq_n, wq_r], axis=1).astype(BF16)
            wuk = jnp.transpose(w_kv_uk[e], (1, 2, 0)).astype(BF16)
            wuv = jnp.transpose(w_kv_uv[e], (1, 0, 2)).astype(BF16)
            wout = w_out_even[e].astype(BF16)
            gq, gkv = g_q_lat[e].reshape(1, -1), g_kv_lat[e].reshape(1, -1)
            lam, gsub = diff_lambda[e], g_diff_subln[e].reshape(1, -1)

            def even_in(h, tabs, tm):
                return _even_in(h, g0, win, gq, gkv, wq, wuk, tabs[0], tabs[1], tm)

            qa, ka, kab, va, vab, qlat, qrp, ckv, ckvb = even_in(hp, tabs_p, tm_p)
            oa = _prompt_attn(functools.partial(_prompt_a_kernel, lam_init), [qa, kab, vab, lam, gsub],
                              [_q_tile_spec(qa, tq), _seq_spec(kab), _seq_spec(vab), _full_spec(lam.shape),
                               _full_spec(gsub.shape)],
                              bp, tp, 512, 512, [(A_DV, 2 * tq)] * A_HEADS, tq, "prompt_attn_a")
            olat = _prompt_attn(_prompt_b_kernel, [qlat, qrp, ckvb],
                                [_q_tile_spec(qlat, tq), _q_tile_spec(qrp, tq), _seq_spec(ckvb)],
                                bp, tp, 1024, KV_LORA, [(KV_LORA, B_HEADS * tq)], tq, "prompt_attn_b")
            mix_p = (oa, olat)
            outs_p["ak"].append(ka.reshape(bp, tp, 2, A_HEADS, A_DH))
            outs_p["av"].append(va.reshape(bp, tp, A_HEADS, A_DV))
            outs_p["ckv"].append(ckv)

            qa, ka, kab, va, vab, qlat, qrp, ckv, ckvb = even_in(hs, tabs_s, tm_s)
            sel = (jnp.arange(8)[:, None] == jnp.arange(8).reshape(2, 4).T.reshape(8)[None, :])
            qx = jnp.transpose(qa.reshape(bs, ts, 2, A_HEADS, A_DH), (0, 3, 2, 1, 4)).reshape(bs, 8, ts, A_DH)
            q_bd = jnp.where(sel.T[None, :, None, :, None], qx[:, :, :, None, :], jnp.zeros((), BF16))
            q_bd = q_bd.reshape(bs, 8 * ts, 512)
            ka3, va3, ckv3 = ka.reshape(bs, ts, 512), va.reshape(bs, ts, 512), ckv.reshape(bs, ts, 320)
            kn = _feature_major_page(ka3)
            vn = _pad_axis(va3.reshape(bs, ts * A_HEADS, A_DV), 1, PAGE * A_HEADS)
            q_abs = jnp.concatenate([qlat.reshape(bs, ts, B_HEADS, KV_LORA),
                                     qrp.reshape(bs, ts, B_HEADS, LANES)[..., :QK_ROPE]], axis=-1)
            q_abs = jnp.swapaxes(q_abs, 1, 2).reshape(bs, B_HEADS * ts, KV_LORA + QK_ROPE)
            oa, olat = _decode_attn(functools.partial(_decode_ab_kernel, n_pp, lam_init), page_table, [q_bd, q_abs],
                                    [akT[e:e + 1], av4[e:e + 1], bcT[e:e + 1]],
                                    [kn, vn, _feature_major_page(ckv3)], [lam, gsub], [512, 1024],
                                    _softmax_scratch(64, A_DV) + _softmax_scratch(32, KV_LORA), n_pp, False,
                                    "decode_attn_ab")
            mix_s = (oa.reshape(1, ns, 512), olat.reshape(1, ns, 1024))
            outs_s["ak"].append(ka.reshape(bs, ts, 2, A_HEADS, A_DH))
            outs_s["av"].append(va.reshape(bs, ts, A_HEADS, A_DV))
            outs_s["ckv"].append(ckv3)

            hp = _mix_out(_even_out_kernel, hp, mix_p, [wuv, wout, g1], tm_p, "even_out_proj")
            hs = _mix_out(_even_out_kernel, hs, mix_s, [wuv, wout, g1], tm_s, "even_out_proj")
        else:
            o = li // 2
            w = w_in_odd[o]
            nq, nk = C_HEADS * C_DH, C_KV_HEADS * C_DH
            wq_perm = w[:, :nq].reshape(d, C_HEADS, C_DH)[:, jnp.array(_HEAD_OF_SLOT), :].reshape(d, nq)
            win = jnp.concatenate([wq_perm, w[:, nq:nq + 2 * nk]], axis=1).astype(BF16)
            wft = jnp.transpose(w[:, nq + 2 * nk:]).astype(BF16)
            bf = b_forget[o].reshape(C_HEADS, 1)
            wout = w_out_odd[o].reshape(C_HEADS, C_DH, d)[jnp.array(_HEAD_OF_SLOT)].reshape(nq, d).astype(BF16)

            q, k, kb, v, vb, lft, ct, cn = _odd_in(hp, g0, win, wft, bf, tm_p)
            op = _prompt_attn(_prompt_c_kernel, [q, ct, kb, vb, cn],
                              [_q_tile_spec(q, tq), pl.BlockSpec((1, C_HEADS, tq), lambda b, i: (b, 0, i)),
                               _seq_spec(kb), _seq_spec(vb), _seq_spec(cn)],
                              bp, tp, 1024, 512, [(C_DH, 2 * tq)] * 2, tq, "prompt_attn_c")
            outs_p["ck"].append(k.reshape(bp, tp, C_KV_HEADS, C_DH))
            outs_p["cv"].append(v.reshape(bp, tp, C_KV_HEADS, C_DH))
            outs_p["cf"].append(jnp.swapaxes(lft, 1, 2))

            q, k, kb, v, vb, lft, ct, cn = _odd_in(hs, g0, win, wft, bf, tm_s)
            qh = q.reshape(bs, ts, C_HEADS, C_DH)[:, :, jnp.array(_SLOT_OF_HEAD), :]
            qh = jnp.swapaxes(qh, 1, 2)
            selc = (jnp.arange(C_HEADS)[:, None] // 2 == jnp.arange(C_KV_HEADS)[None, :])
            q_bd = jnp.where(selc[None, :, None, :, None], qh[:, :, :, None, :], jnp.zeros((), BF16))
            q_bd = q_bd.reshape(bs, C_HEADS * ts, 512)
            k3, v3 = k.reshape(bs, ts, 512), v.reshape(bs, ts, 512)
            lf3 = jnp.swapaxes(lft.reshape(C_HEADS, bs, ts), 0, 1)
            (os_,) = _decode_attn(functools.partial(_decode_c_kernel, n_pp), page_table, [q_bd],
                                  [ckT[o:o + 1], cvT[o:o + 1], cfT[o:o + 1]],
                                  [_feature_major_page(k3), _feature_major_page(v3), _pad_axis(lf3, 2, PAGE)],
                                  [tri_ge, tri_le], [1024],
                                  _softmax_scratch(C_HEADS * 8, 512)
                                  + [pltpu.VMEM((C_HEADS, 1), F32), pltpu.VMEM((C_HEADS * 8, 1), F32)],
                                  n_pp, True, "decode_attn_c")
            outs_s["ck"].append(k3.reshape(bs, ts, C_KV_HEADS, C_DH))
            outs_s["cv"].append(v3.reshape(bs, ts, C_KV_HEADS, C_DH))
            outs_s["cf"].append(jnp.swapaxes(lf3, 1, 2))

            hp = _mix_out(_odd_out_kernel, hp, (op,), [wout, g1], tm_p, "odd_out_proj")
            hs = _mix_out(_odd_out_kernel, hs, (os_.reshape(1, ns, nq),), [wout, g1], tm_s, "odd_out_proj")

        wg, wu, wd = ffn_w_gate[li].astype(BF16), ffn_w_up[li].astype(BF16), ffn_w_down[li].astype(BF16)
        cw, cb = ffn_conv_w[li], ffn_conv_b[li].reshape(1, dff)
        hp, tail = _ffn(hp, g2, wg, wu, cw, cb, wd, g3, None, tm_p, tf)
        outs_p["conv"].append(tail[:, 8 - (CONV_W - 1):])
        buf = state_conv[li]
        prev2 = _pad_axis(buf, 1, ts).reshape(1, ns, dff)
        prev1 = _pad_axis(buf[:, 1:], 1, ts).reshape(1, ns, dff)
        hs, gate = _ffn(hs, g2, wg, wu, cw, cb, wd, g3, (prev2, prev1), tm_s, tf)
        outs_s["conv"].append(gate.reshape(bs, ts, dff)[:, ts - (CONV_W - 1):])

    st = lambda xs: jnp.stack(xs)
    res = [hp, hs.reshape(bs, ts, d)]
    for key in ("ak", "av", "ckv", "ck", "cv", "cf", "conv"):
        res += [st(outs_p[key]), st(outs_s[key])]
    return tuple(res)
```

```python
import functools
import math

import jax
import jax.numpy as jnp
from jax import lax
from jax.experimental import pallas as pl
from jax.experimental.pallas import tpu as pltpu

F32 = jnp.float32
BF16 = jnp.bfloat16

EPS = 1e-6
ROPE_THETA = 500000.0
A_HEADS, A_DH, A_DV, A_ROT = 4, 64, 128, 16
B_HEADS, Q_LORA, KV_LORA, QK_NOPE, QK_ROPE, B_DV = 4, 384, 256, 128, 64, 128
C_HEADS, C_KV_HEADS, C_DH = 16, 8, 64
CONV_W = 3
PAGE = 128
LANES = 128
FLASH_COLS = 128
DECODE_SPLIT = 1
LOG2E = math.log2(math.e)
NEG = -1e30
VMEM_LIMIT_BYTES = 56 * 1024 * 1024

_NT = (((1,), (1,)), ((), ()))


def _cp(sem):
    return pltpu.CompilerParams(dimension_semantics=sem, vmem_limit_bytes=VMEM_LIMIT_BYTES)


def _rms(x, g):
    return x * lax.rsqrt(jnp.mean(x * x, axis=-1, keepdims=True) + EPS) * g


def _dot(a, b):
    return jnp.dot(a, b, preferred_element_type=F32)


def _dot_nt(a, b):
    return lax.dot_general(a, b, _NT, preferred_element_type=F32)


def _rope_chunk(x, c, sl, sr, shift):
    return x * c + pltpu.roll(x, LANES - shift, 1) * sl + pltpu.roll(x, shift, 1) * sr


def _rope_tables(pos, rot):
    half = rot // 2
    inv = jnp.power(jnp.float32(ROPE_THETA), -jnp.arange(half, dtype=F32) / half)
    ang = pos.astype(F32)[:, None] * inv[None, :]
    cos, sin = jnp.cos(ang), jnp.sin(ang)
    t = pos.shape[0]
    one = jnp.ones((t, 64 - rot), F32)
    z_rest = jnp.zeros((t, 64 - rot), F32)
    z_half = jnp.zeros((t, half), F32)
    c = jnp.concatenate([cos, cos, one], axis=1)
    sl = jnp.concatenate([-sin, z_half, z_rest], axis=1)
    sr = jnp.concatenate([z_half, sin, z_rest], axis=1)
    return tuple(jnp.tile(a, (1, 2)) for a in (c, sl, sr))


def _even_in_kernel(logit_unit, h_ref, g0_ref, win_ref, gq_ref, gkv_ref, wq_ref, wuk_ref,
                    ca_ref, sla_ref, sra_ref, cb_ref, slb_ref, srb_ref,
                    qa_ref, ka_ref, kab_ref, va_ref, vab_ref, qlat_ref, qrp_ref, ckv_ref, ckvb_ref):
    hn = _rms(h_ref[0], g0_ref[...]).astype(BF16)
    z = _dot(hn, win_ref[...])
    ca, sla, sra = ca_ref[...], sla_ref[...], sra_ref[...]
    cb, slb, srb = cb_ref[...], slb_ref[...], srb_ref[...]
    for j in range(4):
        sl = slice(LANES * j, LANES * (j + 1))
        qa_ref[0, :, sl] = (_rope_chunk(z[:, sl], ca, sla, sra, A_ROT // 2)
                            * (A_DH ** -0.5 * logit_unit)).astype(BF16)
        kr = _rope_chunk(z[:, 512 + LANES * j:512 + LANES * (j + 1)], ca, sla, sra, A_ROT // 2)
        ka_ref[0, :, sl] = kr
        kab_ref[0, :, sl] = kr.astype(BF16)
    va = z[:, 1024:1536]
    tm = va.shape[0]
    for h in range(A_HEADS):
        va_ref[0, pl.ds(h, tm, stride=A_HEADS), :] = va[:, A_DV * h:A_DV * (h + 1)]
    vab_ref[0] = va.astype(BF16)
    cqn = _rms(z[:, 1536:1920], gq_ref[...]).astype(BF16)
    q2 = _dot(cqn, wq_ref[...])
    for h in range(B_HEADS):
        qn = q2[:, LANES * h:LANES * (h + 1)].astype(BF16)
        qlat_ref[0, :, KV_LORA * h:KV_LORA * (h + 1)] = _dot(qn, wuk_ref[h]).astype(BF16)
        qr = _rope_chunk(q2[:, 512 + LANES * h:512 + LANES * (h + 1)], cb, slb, srb, QK_ROPE // 2)
        qrp_ref[0, :, LANES * h:LANES * (h + 1)] = qr.astype(BF16)
    cn = _rms(z[:, 1920:2176], gkv_ref[...])
    krr = _rope_chunk(z[:, 2176:2304], cb, slb, srb, QK_ROPE // 2)
    ckv_ref[0, :, 0:KV_LORA] = cn
    ckv_ref[0, :, KV_LORA:KV_LORA + QK_ROPE] = krr[:, :QK_ROPE]
    ckvb_ref[0, :, 0:KV_LORA] = cn.astype(BF16)
    ckvb_ref[0, :, KV_LORA:KV_LORA + LANES] = krr.astype(BF16)


def _tab_spec(tm, n_tab_tiles):
    return pl.BlockSpec((tm, LANES), lambda b, i: (i % n_tab_tiles, 0))


def _full_spec(shape):
    nd = len(shape)
    return pl.BlockSpec(shape, lambda b, i: (0,) * nd)


def _even_in(h, g0, win, gq, gkv, wq, wuk, tabs_a, tabs_b, tm, logit_unit):
    bk, tk, d = h.shape
    n_tab = tabs_a[0].shape[0] // tm
    tok = lambda w: pl.BlockSpec((1, tm, w), lambda b, i: (b, i, 0))
    out_shapes = [
        jax.ShapeDtypeStruct((bk, tk, 512), BF16),
        jax.ShapeDtypeStruct((bk, tk, 512), F32),
        jax.ShapeDtypeStruct((bk, tk, 512), BF16),
        jax.ShapeDtypeStruct((bk, tk * A_HEADS, A_DV), F32),
        jax.ShapeDtypeStruct((bk, tk, 512), BF16),
        jax.ShapeDtypeStruct((bk, tk, 1024), BF16),
        jax.ShapeDtypeStruct((bk, tk, 512), BF16),
        jax.ShapeDtypeStruct((bk, tk, 320), F32),
        jax.ShapeDtypeStruct((bk, tk, 384), BF16),
    ]
    return pl.pallas_call(
        functools.partial(_even_in_kernel, logit_unit),
        grid=(bk, tk // tm),
        in_specs=[tok(d), _full_spec(g0.shape), _full_spec(win.shape), _full_spec(gq.shape),
                  _full_spec(gkv.shape), _full_spec(wq.shape), _full_spec(wuk.shape)]
                 + [_tab_spec(tm, n_tab)] * 6,
        out_specs=[pl.BlockSpec((1, tm * (s.shape[1] // tk), s.shape[2]), lambda b, i: (b, i, 0))
                   for s in out_shapes],
        out_shape=out_shapes,
        compiler_params=_cp(("parallel", "parallel")),
        name="even_in_proj",
    )(h, g0, win, gq, gkv, wq, wuk, *tabs_a, *tabs_b)


def _split3(x):
    hi = x.astype(BF16)
    r = x - hi.astype(F32)
    mid = r.astype(BF16)
    lo = (r - mid.astype(F32)).astype(BF16)
    return hi, mid, lo


def _tri_dot(x, tri):
    n = x.shape[0]
    hi, mid, lo = _split3(x)
    y = _dot(jnp.concatenate([hi, mid, lo], axis=0), tri)
    return y[0:n] + y[n:2 * n] + y[2 * n:3 * n]


def _log_sigmoid(x):
    return -(jnp.maximum(-x, 0.0) + jnp.log1p(jnp.exp(-jnp.abs(x))))


def _odd_in_kernel(logit_unit, h_ref, g0_ref, win_ref, wft_ref, bf_ref, tri_ref,
                   q_ref, k_ref, kb_ref, v_ref, vb_ref, lft_ref, ct_ref, cn_ref, carry_ref):
    i = pl.program_id(1)
    hn = _rms(h_ref[0], g0_ref[...]).astype(BF16)
    z = _dot(hn, win_ref[...])
    q_ref[0] = (z[:, 0:1024] * (C_DH ** -0.5 * logit_unit)).astype(BF16)
    k = z[:, 1024:1536]
    k_ref[0] = k
    kb_ref[0] = k.astype(BF16)
    v = z[:, 1536:2048]
    v_ref[0] = v
    vb_ref[0] = v.astype(BF16)
    ft = _dot_nt(wft_ref[...], hn)
    lft = _log_sigmoid(ft + bf_ref[...])
    lft_ref[0] = lft

    @pl.when(i == 0)
    def _():
        carry_ref[...] = jnp.zeros_like(carry_ref)

    c = _tri_dot(lft, tri_ref[...]) + carry_ref[...]
    carry_ref[...] = c[:, -1:]
    c = c * logit_unit
    ct_ref[0] = c
    c128 = jnp.concatenate([c, jnp.zeros((LANES - C_HEADS, c.shape[1]), F32)], axis=0)
    cn_ref[0] = jnp.transpose(c128)[:, 0:C_HEADS]


def _odd_in(h, g0, win, wft, bf, tm, logit_unit):
    bk, tk, d = h.shape
    tri = jnp.triu(jnp.ones((tm, tm), F32)).astype(BF16)
    tok = lambda w: pl.BlockSpec((1, tm, w), lambda b, i: (b, i, 0))
    tokt = pl.BlockSpec((1, C_HEADS, tm), lambda b, i: (b, 0, i))
    out_shapes = [
        jax.ShapeDtypeStruct((bk, tk, 1024), BF16),
        jax.ShapeDtypeStruct((bk, tk, 512), F32),
        jax.ShapeDtypeStruct((bk, tk, 512), BF16),
        jax.ShapeDtypeStruct((bk, tk, 512), F32),
        jax.ShapeDtypeStruct((bk, tk, 512), BF16),
        jax.ShapeDtypeStruct((bk, C_HEADS, tk), F32),
        jax.ShapeDtypeStruct((bk, C_HEADS, tk), F32),
        jax.ShapeDtypeStruct((bk, tk, C_HEADS), F32),
    ]
    return pl.pallas_call(
        functools.partial(_odd_in_kernel, logit_unit),
        grid=(bk, tk // tm),
        in_specs=[tok(d), _full_spec(g0.shape), _full_spec(win.shape), _full_spec(wft.shape),
                  _full_spec(bf.shape), _full_spec(tri.shape)],
        out_specs=[tok(1024), tok(512), tok(512), tok(512), tok(512), tokt, tokt, tok(C_HEADS)],
        out_shape=out_shapes,
        scratch_shapes=[pltpu.VMEM((C_HEADS, 1), F32)],
        compiler_params=_cp(("parallel", "arbitrary")),
        name="odd_in_proj",
    )(h, g0, win, wft, bf, tri)


def _even_out_kernel(h_ref, oa_ref, olat_ref, wuv_ref, wout_ref, g1_ref, out_ref):
    parts = [oa_ref[0]]
    for hd in range(B_HEADS):
        parts.append(_dot(olat_ref[0, :, KV_LORA * hd:KV_LORA * (hd + 1)], wuv_ref[hd]).astype(BF16))
    y = _dot(jnp.concatenate(parts, axis=1), wout_ref[...])
    out_ref[0] = h_ref[0] + _rms(y, g1_ref[...])


def _odd_out_kernel(h_ref, o_ref, wout_ref, g1_ref, out_ref):
    y = _dot(o_ref[0], wout_ref[...])
    out_ref[0] = h_ref[0] + _rms(y, g1_ref[...])


def _mix_out(kern, h, acts, consts, tm, name):
    bk, tk, d = h.shape
    tok = lambda w: pl.BlockSpec((1, tm, w), lambda b, i: (b, i, 0))
    return pl.pallas_call(
        kern,
        grid=(bk, tk // tm),
        in_specs=[tok(d)] + [tok(a.shape[-1]) for a in acts] + [_full_spec(c.shape) for c in consts],
        out_specs=tok(d),
        out_shape=jax.ShapeDtypeStruct(h.shape, F32),
        compiler_params=_cp(("parallel", "parallel")),
        name=name,
    )(h, *acts, *consts)


def _ffn_kernel(short_seq, *refs):
    if short_seq:
        (h_ref, g2_ref, wg_ref, wu_ref, cw_ref, cb_ref, wd_ref, g3_ref, pa_ref, pb_ref,
         out_ref, gate_ref, hn_ref, acc_ref) = refs
    else:
        (h_ref, g2_ref, wg_ref, wu_ref, cw_ref, cb_ref, wd_ref, g3_ref,
         out_ref, tail_ref, hn_ref, acc_ref, carry_ref) = refs
    i = pl.program_id(1)
    c = pl.program_id(2)

    @pl.when(c == 0)
    def _():
        hn_ref[...] = _rms(h_ref[0], g2_ref[...]).astype(BF16)
        acc_ref[...] = jnp.zeros_like(acc_ref)

    hn = hn_ref[...]
    g = _dot(hn, wg_ref[...])
    u = _dot(hn, wu_ref[...])
    tm = g.shape[0]
    row = lax.broadcasted_iota(jnp.int32, g.shape, 0)
    r1 = pltpu.roll(g, 1, 0)
    r2 = pltpu.roll(g, 2, 0)
    if short_seq:
        t = row % 8
        gm1 = jnp.where(t == 0, pb_ref[0], r1)
        gm2 = jnp.where(t < 2, pa_ref[0], r2)
        gate_ref[0] = g
    else:
        top = jnp.where(i == 0, 0.0, carry_ref[c])
        gm1 = jnp.where(row == 0, top[7:8], r1)
        gm2 = jnp.where(row == 0, top[6:7], jnp.where(row == 1, top[7:8], r2))
        carry_ref[c] = g[tm - 8:tm]
        tail_ref[0] = g[tm - 8:tm]
    cw = cw_ref[...]
    gc = cb_ref[...] + ((cw[0:1] * gm2 + cw[1:2] * gm1) + cw[2:3] * g)
    act = (jax.nn.silu(gc) * u).astype(BF16)
    acc_ref[...] += _dot(act, wd_ref[...])

    @pl.when(c == pl.num_programs(2) - 1)
    def _():
        out_ref[0] = h_ref[0] + _rms(acc_ref[...], g3_ref[...])


def _ffn(h, g2, wg, wu, cw, cb, wd, g3, prev, tm, tf):
    bk, tk, d = h.shape
    dff = wg.shape[1]
    nc = dff // tf
    short_seq = prev is not None
    tok = pl.BlockSpec((1, tm, d), lambda b, i, c: (b, i, 0))
    const = lambda shape: pl.BlockSpec(shape, lambda b, i, c: (0,) * len(shape))
    in_specs = [tok, const(g2.shape),
                pl.BlockSpec((d, tf), lambda b, i, c: (0, c)),
                pl.BlockSpec((d, tf), lambda b, i, c: (0, c)),
                pl.BlockSpec((CONV_W, tf), lambda b, i, c: (0, c)),
                pl.BlockSpec((1, tf), lambda b, i, c: (0, c)),
                pl.BlockSpec((tf, d), lambda b, i, c: (c, 0)),
                const(g3.shape)]
    args = [h, g2, wg, wu, cw, cb, wd, g3]
    scratch = [pltpu.VMEM((tm, d), BF16), pltpu.VMEM((tm, d), F32)]
    if short_seq:
        in_specs += [pl.BlockSpec((1, tm, tf), lambda b, i, c: (b, i, c))] * 2
        args += list(prev)
        out_specs = [tok, pl.BlockSpec((1, tm, tf), lambda b, i, c: (b, i, c))]
        out_shape = [jax.ShapeDtypeStruct(h.shape, F32), jax.ShapeDtypeStruct((bk, tk, dff), F32)]
    else:
        out_specs = [tok, pl.BlockSpec((1, 8, tf), lambda b, i, c: (b, 0, c))]
        out_shape = [jax.ShapeDtypeStruct(h.shape, F32), jax.ShapeDtypeStruct((bk, 8, dff), F32)]
        scratch.append(pltpu.VMEM((nc, 8, tf), F32))
    return pl.pallas_call(
        functools.partial(_ffn_kernel, short_seq),
        grid=(bk, tk // tm, nc),
        in_specs=in_specs,
        out_specs=out_specs,
        out_shape=out_shape,
        scratch_shapes=scratch,
        compiler_params=_cp(("parallel", "arbitrary", "arbitrary")),
        name="conv_ffn",
    )(*args)


def _flash_cols(i, tq, chains):
    subs = []
    for c in chains:
        c["acc"][...] = jnp.zeros_like(c["acc"])
        for c0 in range(0, c["acc"].shape[1], c["cw"]):
            subs.append((c, slice(c0, c0 + c["cw"])))

    def step(j, stats, masked):
        kv = pl.ds(pl.multiple_of(j * tq, tq), tq)
        new_stats = []
        for (c, cols), (m_prev, l_prev) in zip(subs, stats):
            s = c["qk"](kv, cols)
            if c["scale"] is not None:
                s = s * c["scale"]
            if c["bias"] is not None:
                s = s + c["bias"](kv, cols)
            if masked:
                key = lax.broadcasted_iota(jnp.int32, s.shape, 0)
                qpos = (lax.broadcasted_iota(jnp.int32, s.shape, 1) + cols.start) % tq
                s = jnp.where(key <= qpos, s, NEG)
            m_new = jnp.maximum(m_prev, jnp.max(s, axis=0, keepdims=True))
            alpha = jnp.exp2(m_prev - m_new)
            p = jnp.exp2(s - m_new)
            c["acc"][:, cols] = alpha * c["acc"][:, cols] + _dot(c["vt"](kv), p.astype(BF16))
            new_stats.append((m_new, alpha * l_prev + jnp.sum(p, axis=0, keepdims=True)))
        return tuple(new_stats)

    init = tuple((jnp.full((1, c["cw"]), NEG, F32), jnp.zeros((1, c["cw"]), F32)) for c, _ in subs)
    stats = lax.fori_loop(0, i, lambda j, st: step(j, st, False), init)
    stats = step(i, stats, True)
    outs, k = [], 0
    for c in chains:
        n = c["acc"].shape[1] // c["cw"]
        l = jnp.concatenate([stats[k + t][1] for t in range(n)], axis=1)
        outs.append(c["acc"][...] / l)
        k += n
    return outs


def _transpose_bf16(x):
    return jnp.transpose(x.astype(F32)).astype(BF16)


def _fill_transposed(i, tq, src_fn, dst_ref):
    n_tiles = dst_ref.shape[1] // tq

    @pl.when(i == 0)
    def _():
        def body(c, carry):
            rows = pl.ds(pl.multiple_of(c * tq, tq), tq)
            dst_ref[:, rows] = _transpose_bf16(src_fn(rows))
            return carry

        lax.fori_loop(0, n_tiles, body, 0)


def _chain_scratch(dv, r):
    return [pltpu.VMEM((dv, r), F32)]


def _lambda_value(lam_ref, lam_init):
    lam = lam_ref[...]
    s01 = jnp.sum(lam[0:1] * lam[1:2], axis=-1, keepdims=True)
    s23 = jnp.sum(lam[2:3] * lam[3:4], axis=-1, keepdims=True)
    return jnp.exp(s01) - jnp.exp(s23) + lam_init


def _half_rows(blk, half):
    z = jnp.zeros_like(blk)
    return jnp.concatenate([blk, z] if half == 0 else [z, blk], axis=0)


def _prompt_a_kernel(lam_init, q_ref, k_ref, v_ref, lam_ref, gsub_ref, out_ref, vt_ref, *st):
    i = pl.program_id(1)
    tq = q_ref.shape[1]
    _fill_transposed(i, tq, lambda rows: v_ref[0, rows, :], vt_ref)
    qt = _transpose_bf16(q_ref[0])
    lam_val = _lambda_value(lam_ref, lam_init)
    for pair in range(A_HEADS // 2):
        chains = []
        for hd in (2 * pair, 2 * pair + 1):
            slots = [s * A_HEADS + hd for s in range(2)]
            qts = [_half_rows(qt[64 * e:64 * (e + 1)], e % 2) for e in slots]
            ksl = [slice(LANES * (e // 2), LANES * (e // 2 + 1)) for e in slots]

            def qk(kv, cols, qts=qts, ksl=ksl):
                s, lo = divmod(cols.start, tq)
                return _dot(k_ref[0, kv, ksl[s]], qts[s][:, lo:lo + cols.stop - cols.start])

            def vt(kv, hd=hd):
                return vt_ref[A_DV * hd:A_DV * (hd + 1), kv]

            chains.append(dict(qk=qk, vt=vt, scale=None, bias=None, acc=st[hd], cw=FLASH_COLS))
        for hd, ot in zip((2 * pair, 2 * pair + 1), _flash_cols(i, tq, chains)):
            d = jnp.transpose(ot[:, 0:tq] - lam_val * ot[:, tq:2 * tq])
            out_ref[0, :, A_DV * hd:A_DV * (hd + 1)] = (_rms(d, gsub_ref[...]) * (1.0 - lam_init)).astype(BF16)


def _prompt_b_kernel(q1_ref, q2_ref, c_ref, out_ref, vt_ref, acc_ref):
    i = pl.program_id(1)
    tq = q1_ref.shape[1]
    _fill_transposed(i, tq, lambda rows: c_ref[0, rows, 0:KV_LORA], vt_ref)
    q1t = _transpose_bf16(q1_ref[0])
    q2t = _transpose_bf16(q2_ref[0])
    q1s = jnp.concatenate([q1t[KV_LORA * hd:KV_LORA * (hd + 1)] for hd in range(B_HEADS)], axis=1)
    q2s = jnp.concatenate([q2t[LANES * hd:LANES * (hd + 1)] for hd in range(B_HEADS)], axis=1)

    def qk(kv, cols):
        return (_dot(c_ref[0, kv, 0:KV_LORA], q1s[:, cols])
                + _dot(c_ref[0, kv, KV_LORA:KV_LORA + LANES], q2s[:, cols]))

    chain = dict(qk=qk, vt=lambda kv: vt_ref[:, kv], scale=(QK_NOPE + QK_ROPE) ** -0.5 * LOG2E, bias=None,
                 acc=acc_ref,
                 cw=B_HEADS * tq)
    (ot,) = _flash_cols(i, tq, [chain])
    for hd in range(B_HEADS):
        out_ref[0, :, KV_LORA * hd:KV_LORA * (hd + 1)] = jnp.transpose(ot[:, tq * hd:tq * (hd + 1)]).astype(BF16)


def _prompt_c_kernel(q_ref, ct_ref, k_ref, v_ref, cn_ref, out_ref, vt_ref, *st):
    i = pl.program_id(1)
    tq = q_ref.shape[1]
    _fill_transposed(i, tq, lambda rows: v_ref[0, rows, :], vt_ref)
    qt = _transpose_bf16(q_ref[0])
    n_pairs = len(st) // 2
    for pp0 in range(0, C_KV_HEADS // 2, n_pairs):
        chains = []
        for pp in range(pp0, pp0 + n_pairs):
            for half in range(2):
                heads = [4 * pp + 2 * half + r for r in range(2)]
                blk = jnp.concatenate(
                    [qt[LANES * (2 * pp + r) + 64 * half:LANES * (2 * pp + r) + 64 * (half + 1)]
                     for r in range(2)], axis=1)
                qs = _half_rows(blk, half)
                cq = [ct_ref[0, hd:hd + 1, :] for hd in heads]

                def bias(kv, cols, heads=heads, cq=cq):
                    return jnp.concatenate([cq[r] - cn_ref[0, kv, hd:hd + 1] for r, hd in enumerate(heads)],
                                           axis=1)

                def qk(kv, cols, qs=qs, pp=pp):
                    return _dot(k_ref[0, kv, LANES * pp:LANES * (pp + 1)], qs[:, cols])

                def vt(kv, g=2 * pp + half):
                    return vt_ref[C_DH * g:C_DH * (g + 1), kv]

                chains.append(dict(qk=qk, vt=vt, scale=None, bias=bias, acc=st[2 * (pp - pp0) + half],
                                   cw=2 * tq))
        outs = _flash_cols(i, tq, chains)
        for pp in range(pp0, pp0 + n_pairs):
            ot = jnp.concatenate(outs[2 * (pp - pp0):2 * (pp - pp0) + 2], axis=0)
            for r in range(2):
                out_ref[0, :, LANES * (2 * pp + r):LANES * (2 * pp + r + 1)] = jnp.transpose(
                    ot[:, tq * r:tq * (r + 1)]).astype(BF16)


def _prompt_attn(kern, args, in_specs, bk, t, out_w, vt_rows, chain_shapes, tq, name):
    scratch = [pltpu.VMEM((vt_rows, t), BF16)]
    for dv, r in chain_shapes:
        scratch += _chain_scratch(dv, r)
    return pl.pallas_call(
        kern,
        grid=(bk, t // tq),
        in_specs=in_specs,
        out_specs=pl.BlockSpec((1, tq, out_w), lambda b, i: (b, i, 0)),
        out_shape=jax.ShapeDtypeStruct((bk, t, out_w), BF16),
        scratch_shapes=scratch,
        compiler_params=_cp(("parallel", "arbitrary")),
        name=name,
    )(*args)


def _q_tile_spec(a, tq):
    return pl.BlockSpec((1, tq, a.shape[-1]), lambda b, i: (b, i, 0))


def _seq_spec(a):
    return pl.BlockSpec((1,) + a.shape[1:], lambda b, i: (b, 0, 0))


def _online_update(s_list, pv_fn, m_ref, l_ref, acc_ref):
    m_prev = m_ref[...]
    m_new = m_prev
    for s in s_list:
        m_new = jnp.maximum(m_new, jnp.max(s, axis=-1, keepdims=True))
    alpha = jnp.exp(m_prev - m_new)
    ps = [jnp.exp(s - m_new) for s in s_list]
    l_new = alpha * l_ref[...]
    for p in ps:
        l_new = l_new + jnp.sum(p, axis=-1, keepdims=True)
    l_ref[...] = l_new
    acc_ref[...] = alpha * acc_ref[...] + pv_fn([p.astype(BF16) for p in ps])
    m_ref[...] = m_new


def _page_groups(n_pp):
    per = -(-n_pp // DECODE_SPLIT)
    return [list(range(g * per, min((g + 1) * per, n_pp))) for g in range(DECODE_SPLIT)]


def _merged_softmax(states):
    ms = [m[...] for m, _, _ in states]
    m_all = functools.reduce(jnp.maximum, ms)
    num, den = None, None
    for m, (_, l_ref, acc_ref) in zip(ms, states):
        w = jnp.exp(m - m_all)
        num = acc_ref[...] * w if num is None else num + acc_ref[...] * w
        den = l_ref[...] * w if den is None else den + l_ref[...] * w
    return num / den


def _init_softmax(m_ref, l_ref, acc_ref):
    m_ref[...] = jnp.full_like(m_ref, NEG)
    l_ref[...] = jnp.zeros_like(l_ref)
    acc_ref[...] = jnp.zeros_like(acc_ref)


def _new_token_mask(s):
    row = lax.broadcasted_iota(jnp.int32, s.shape, 0) % 8
    col = lax.broadcasted_iota(jnp.int32, s.shape, 1)
    return jnp.where(col <= row, s, NEG)


def _decode_ab_kernel(n_pp, lam_init, pt_ref, qa_ref, qb_ref, *refs):
    ak_refs, av_refs, bc_refs = refs[:n_pp], refs[n_pp:2 * n_pp], refs[2 * n_pp:3 * n_pp]
    kn_ref, vn_ref, cn_ref, lam_ref, gsub_ref, oa_ref, ob_ref = refs[3 * n_pp:3 * n_pp + 7]
    st = refs[3 * n_pp + 7:]
    sa = [st[3 * g:3 * g + 3] for g in range(DECODE_SPLIT)]
    sb = [st[3 * (DECODE_SPLIT + g):3 * (DECODE_SPLIT + g) + 3] for g in range(DECODE_SPLIT)]
    j = pl.program_id(1)
    qa = qa_ref[0]
    qb = qb_ref[0]
    scale_b = (QK_NOPE + QK_ROPE) ** -0.5

    def update_a(state, k_pages, v_of, mask):
        s_list = [_dot(qa, k) for k in k_pages]
        if mask:
            s_list = [_new_token_mask(s) for s in s_list]

        def pv(ps):
            outs = []
            for hd in range(A_HEADS):
                o = None
                for p, pr in enumerate(ps):
                    t = _dot(pr[16 * hd:16 * (hd + 1)], v_of(p, hd))
                    o = t if o is None else o + t
                outs.append(o)
            return jnp.concatenate(outs, axis=0)

        _online_update(s_list, pv, *state)

    def update_b(state, pages, mask):
        s_list = [_dot(qb, c) * scale_b for c in pages]
        if mask:
            s_list = [_new_token_mask(s) for s in s_list]

        def pv(ps):
            o = None
            for pr, c in zip(ps, pages):
                t = _dot_nt(pr, c[0:KV_LORA])
                o = t if o is None else o + t
            return o

        _online_update(s_list, pv, *state)

    @pl.when(j == 0)
    def _():
        for state in sa + sb:
            _init_softmax(*state)

    for g, grp in enumerate(_page_groups(n_pp)):
        if not grp:
            continue
        update_a(sa[g], [ak_refs[p][0, 0].astype(BF16) for p in grp],
                 lambda p, hd, grp=grp: av_refs[grp[p]][0, 0, pl.ds(hd, PAGE, stride=A_HEADS), :].astype(BF16),
                 False)
        update_b(sb[g], [bc_refs[p][0, 0].astype(BF16) for p in grp], False)

    @pl.when(j == pl.num_programs(1) - 1)
    def _():
        update_a(sa[0], [kn_ref[0].astype(BF16)],
                 lambda p, hd: vn_ref[0, pl.ds(hd, PAGE, stride=A_HEADS), :].astype(BF16), True)
        update_b(sb[0], [cn_ref[0].astype(BF16)], True)
        o = _merged_softmax(sa)
        lam_val = _lambda_value(lam_ref, lam_init)
        for hd in range(A_HEADS):
            d = o[16 * hd:16 * hd + 8] - lam_val * o[16 * hd + 8:16 * hd + 16]
            oa_ref[0, :, A_DV * hd:A_DV * (hd + 1)] = (
                _rms(d, gsub_ref[...]) * (1.0 - lam_init)).astype(BF16)
        o = _merged_softmax(sb)
        for hd in range(B_HEADS):
            ob_ref[0, :, KV_LORA * hd:KV_LORA * (hd + 1)] = o[8 * hd:8 * (hd + 1)].astype(BF16)


def _expand_heads(x):
    return jnp.broadcast_to(x[:, None, :], (C_HEADS, 8, x.shape[-1])).reshape(C_HEADS * 8, x.shape[-1])


def _decode_c_kernel(n_pp, pt_ref, q_ref, *refs):
    k_refs, v_refs, lf_refs = refs[:n_pp], refs[n_pp:2 * n_pp], refs[2 * n_pp:3 * n_pp]
    kn_ref, vn_ref, lfn_ref, tri_ge_ref, tri_le_ref, out_ref = refs[3 * n_pp:3 * n_pp + 6]
    st = refs[3 * n_pp + 6:]
    states = [st[3 * g:3 * g + 3] for g in range(DECODE_SPLIT)]
    base_ref, bq_ref = st[3 * DECODE_SPLIT:]
    j = pl.program_id(1)
    last = pl.num_programs(1) - 1
    q = q_ref[0]

    def scores(k_page):
        return _dot(q, k_page)

    def pv_of(pages):
        def pv(ps):
            o = None
            for pr, v in zip(ps, pages):
                t = _dot_nt(pr, v)
                o = t if o is None else o + t
            return o
        return pv

    @pl.when(j == 0)
    def _():
        for state in states:
            _init_softmax(*state)
        base_ref[...] = jnp.zeros_like(base_ref)
        cnl = _tri_dot(lfn_ref[0], tri_le_ref[...])
        rep = _expand_heads(cnl)
        row = lax.broadcasted_iota(jnp.int32, rep.shape, 0) % 8
        col = lax.broadcasted_iota(jnp.int32, rep.shape, 1)
        bq = jnp.sum(jnp.where(col == row, rep, 0.0), axis=-1, keepdims=True)
        bq_ref[...] = bq
        s_new = _new_token_mask(scores(kn_ref[0].astype(BF16)) + (bq - rep))
        _online_update([s_new], pv_of([vn_ref[0].astype(BF16)]), *states[0])

    lf = jnp.concatenate([lf_refs[p][0, 0] for p in range(n_pp)], axis=0)
    incl = _tri_dot(lf, tri_ge_ref[...])
    excl = incl - lf
    bq = bq_ref[...]
    base = base_ref[...]
    s_list = []
    for p in range(n_pp):
        sl = slice(C_HEADS * p, C_HEADS * (p + 1))
        bias = _expand_heads(base + excl[sl])
        s_list.append(scores(k_refs[p][0, 0].astype(BF16)) + (bq + bias))
        base = base + incl[sl, 0:1]
    base_ref[...] = base
    for g, grp in enumerate(_page_groups(n_pp)):
        if grp:
            _online_update([s_list[p] for p in grp], pv_of([v_refs[p][0, 0].astype(BF16) for p in grp]),
                           *states[g])

    @pl.when(j == last)
    def _():
        o = _merged_softmax(states)
        lane = lax.broadcasted_iota(jnp.int32, (8, LANES), 1)
        for pp in range(C_KV_HEADS // 2):
            csl = slice(LANES * pp, LANES * (pp + 1))
            for r in range(2):
                h0, h1 = 4 * pp + r, 4 * pp + 2 + r
                chunk = jnp.where(lane < 64, o[8 * h0:8 * (h0 + 1), csl], o[8 * h1:8 * (h1 + 1), csl])
                out_ref[0, :, LANES * (2 * pp + r):LANES * (2 * pp + r + 1)] = chunk.astype(BF16)


def _softmax_scratch(rows, dv):
    return [pltpu.VMEM((rows, 1), F32), pltpu.VMEM((rows, 1), F32), pltpu.VMEM((rows, dv), F32)]


def _decode_attn(kern, page_table, qs, caches, news, consts, out_ws, scratch, n_pp, reverse, name):
    nb, n_pages = page_table.shape
    n_steps = n_pages // n_pp

    def page_spec(c, p):
        def imap(b, j, pt):
            idx = j * n_pp + p
            if reverse:
                idx = n_pages - 1 - idx
            return (0, pt[b, idx], 0, 0)
        return pl.BlockSpec((1, 1) + c.shape[2:], imap)

    in_specs, args = [], []
    for q in qs:
        in_specs.append(pl.BlockSpec((1,) + q.shape[1:], lambda b, j, pt: (b, 0, 0)))
        args.append(q)
    for c in caches:
        for p in range(n_pp):
            in_specs.append(page_spec(c, p))
            args.append(c)
    for a in news:
        in_specs.append(pl.BlockSpec((1,) + a.shape[1:], lambda b, j, pt: (b, 0, 0)))
        args.append(a)
    for a in consts:
        in_specs.append(pl.BlockSpec(a.shape, lambda b, j, pt, nd=a.ndim: (0,) * nd))
        args.append(a)
    grid_spec = pltpu.PrefetchScalarGridSpec(
        num_scalar_prefetch=1,
        grid=(nb, n_steps),
        in_specs=in_specs,
        out_specs=[pl.BlockSpec((1, 8, w), lambda b, j, pt: (b, 0, 0)) for w in out_ws],
        scratch_shapes=scratch,
    )
    return pl.pallas_call(
        kern,
        grid_spec=grid_spec,
        out_shape=[jax.ShapeDtypeStruct((nb, 8, w), BF16) for w in out_ws],
        compiler_params=_cp(("parallel", "arbitrary")),
        name=name,
    )(page_table, *args)


_HEAD_OF_SLOT = [4 * (c // 2) + 2 * half + (c % 2) for c in range(C_HEADS // 2) for half in range(2)]
_SLOT_OF_HEAD = [_HEAD_OF_SLOT.index(h) for h in range(C_HEADS)]


def _lambda_init(layer):
    return 0.8 - 0.6 * math.exp(-0.3 * layer)


def _pad_axis(x, axis, size):
    pad = [(0, 0)] * x.ndim
    pad[axis] = (0, size - x.shape[axis])
    return jnp.pad(x, pad)


def _tok_tile(n, pref):
    tm = min(n, pref)
    assert n % tm == 0 and tm % 8 == 0
    return tm


def _feature_major_page(x):
    return _pad_axis(jnp.swapaxes(x, 1, 2), 2, PAGE)


def kernel(x_prompt, x_sample, cache_a_k, cache_a_v, cache_b_ckv, cache_c_k, cache_c_v, cache_c_logf, state_conv, page_table, norm_gains, w_in_even, w_q_up, w_kv_uk, w_kv_uv, g_q_lat, g_kv_lat, diff_lambda, g_diff_subln, w_out_even, w_in_odd, b_forget, w_out_odd, ffn_w_gate, ffn_w_up, ffn_conv_w, ffn_conv_b, ffn_w_down):
    depth = norm_gains.shape[0]
    bp, tp, d = x_prompt.shape
    bs, ts, _ = x_sample.shape
    n_pages = page_table.shape[1]
    n_pool = cache_a_k.shape[1]
    dff = ffn_w_gate.shape[2]
    assert ts == 8 and cache_a_k.shape[2] == PAGE
    ns = bs * ts
    tm_p, tm_s = _tok_tile(tp, 512), _tok_tile(ns, 512)
    tq = _tok_tile(tp, 256)
    tf = dff // 2 if dff % (2 * LANES) == 0 else dff
    n_pp = 16 if n_pages % 16 == 0 else n_pages
    assert dff % tf == 0

    pos_p = jnp.arange(tp, dtype=jnp.int32)
    pos_s = n_pages * PAGE + jnp.arange(ts, dtype=jnp.int32)
    tabs_p = (_rope_tables(pos_p, A_ROT), _rope_tables(pos_p, QK_ROPE))
    tabs_s = tuple(tuple(jnp.tile(t, (tm_s // ts, 1)) for t in tabs)
                   for tabs in (_rope_tables(pos_s, A_ROT), _rope_tables(pos_s, QK_ROPE)))

    akT = jnp.transpose(cache_a_k, (0, 1, 3, 4, 5, 2)).reshape(-1, n_pool, 512, PAGE)
    av4 = cache_a_v.reshape(-1, n_pool, PAGE * A_HEADS, A_DV)
    bcT = jnp.swapaxes(cache_b_ckv, 2, 3)
    ckT = jnp.transpose(cache_c_k, (0, 1, 3, 4, 2)).reshape(-1, n_pool, 512, PAGE)
    cvT = jnp.transpose(cache_c_v, (0, 1, 3, 4, 2)).reshape(-1, n_pool, 512, PAGE)
    cfT = jnp.swapaxes(cache_c_logf, 2, 3)

    tri_ge = jnp.tril(jnp.ones((PAGE, PAGE), F32)).astype(BF16)
    tri_le = jnp.triu(jnp.ones((PAGE, PAGE), F32)).astype(BF16)

    hp = x_prompt
    hs = x_sample.reshape(1, ns, d)
    outs_p = {k: [] for k in ("ak", "av", "ckv", "ck", "cv", "cf", "conv")}
    outs_s = {k: [] for k in outs_p}

    for li in range(depth):
        gn = norm_gains[li]
        g0, g1, g2, g3 = (gn[k].reshape(1, d) for k in range(4))
        if li % 2 == 0:
            e = li // 2
            lam_init = _lambda_init(li)
            win = _pad_axis(w_in_even[e], 1, 2304).astype(BF16)
            wq_n = w_q_up[e][:, :, :QK_NOPE].reshape(Q_LORA, B_HEADS * QK_NOPE)
            wq_r = _pad_axis(w_q_up[e][:, :, QK_NOPE:], 2, LANES).reshape(Q_LORA, B_HEADS * LANES)
            wq = jnp.concatenate([wq_n, wq_r], axis=1).astype(BF16)
            wuk = jnp.transpose(w_kv_uk[e], (1, 2, 0)).astype(BF16)
            wuv = jnp.transpose(w_kv_uv[e], (1, 0, 2)).astype(BF16)
            wout = w_out_even[e].astype(BF16)
            gq, gkv = g_q_lat[e].reshape(1, -1), g_kv_lat[e].reshape(1, -1)
            lam, gsub = diff_lambda[e], g_diff_subln[e].reshape(1, -1)

            def even_in(h, tabs, tm, logit_unit):
                return _even_in(h, g0, win, gq, gkv, wq, wuk, tabs[0], tabs[1], tm, logit_unit)

            qa, ka, kab, va, vab, qlat, qrp, ckv, ckvb = even_in(hp, tabs_p, tm_p, LOG2E)
            oa = _prompt_attn(functools.partial(_prompt_a_kernel, lam_init), [qa, kab, vab, lam, gsub],
                              [_q_tile_spec(qa, tq), _seq_spec(kab), _seq_spec(vab), _full_spec(lam.shape),
                               _full_spec(gsub.shape)],
                              bp, tp, 512, 512, [(A_DV, 2 * tq)] * A_HEADS, tq, "prompt_attn_a")
            olat = _prompt_attn(_prompt_b_kernel, [qlat, qrp, ckvb],
                                [_q_tile_spec(qlat, tq), _q_tile_spec(qrp, tq), _seq_spec(ckvb)],
                                bp, tp, 1024, KV_LORA, [(KV_LORA, B_HEADS * tq)], tq, "prompt_attn_b")
            mix_p = (oa, olat)
            outs_p["ak"].append(ka.reshape(bp, tp, 2, A_HEADS, A_DH))
            outs_p["av"].append(va.reshape(bp, tp, A_HEADS, A_DV))
            outs_p["ckv"].append(ckv)

            qa, ka, kab, va, vab, qlat, qrp, ckv, ckvb = even_in(hs, tabs_s, tm_s, 1.0)
            sel = (jnp.arange(8)[:, None] == jnp.arange(8).reshape(2, 4).T.reshape(8)[None, :])
            qx = jnp.transpose(qa.reshape(bs, ts, 2, A_HEADS, A_DH), (0, 3, 2, 1, 4)).reshape(bs, 8, ts, A_DH)
            q_bd = jnp.where(sel.T[None, :, None, :, None], qx[:, :, :, None, :], jnp.zeros((), BF16))
            q_bd = q_bd.reshape(bs, 8 * ts, 512)
            ckv3 = ckv.reshape(bs, ts, 320)
            kn = _feature_major_page(kab.reshape(bs, ts, 512))
            vn = _pad_axis(va.reshape(bs, ts * A_HEADS, A_DV), 1, PAGE * A_HEADS)
            cn_page = _feature_major_page(ckvb.reshape(bs, ts, KV_LORA + LANES)[:, :, :KV_LORA + QK_ROPE])
            q_abs = jnp.concatenate([qlat.reshape(bs, ts, B_HEADS, KV_LORA),
                                     qrp.reshape(bs, ts, B_HEADS, LANES)[..., :QK_ROPE]], axis=-1)
            q_abs = jnp.swapaxes(q_abs, 1, 2).reshape(bs, B_HEADS * ts, KV_LORA + QK_ROPE)
            oa, olat = _decode_attn(functools.partial(_decode_ab_kernel, n_pp, lam_init), page_table, [q_bd, q_abs],
                                    [akT[e:e + 1], av4[e:e + 1], bcT[e:e + 1]],
                                    [kn, vn, cn_page], [lam, gsub], [512, 1024],
                                    _softmax_scratch(64, A_DV) * DECODE_SPLIT
                                    + _softmax_scratch(32, KV_LORA) * DECODE_SPLIT, n_pp, False,
                                    "decode_attn_ab")
            mix_s = (oa.reshape(1, ns, 512), olat.reshape(1, ns, 1024))
            outs_s["ak"].append(ka.reshape(bs, ts, 2, A_HEADS, A_DH))
            outs_s["av"].append(va.reshape(bs, ts, A_HEADS, A_DV))
            outs_s["ckv"].append(ckv3)

            hp = _mix_out(_even_out_kernel, hp, mix_p, [wuv, wout, g1], tm_p, "even_out_proj")
            hs = _mix_out(_even_out_kernel, hs, mix_s, [wuv, wout, g1], tm_s, "even_out_proj")
        else:
            o = li // 2
            w = w_in_odd[o]
            nq, nk = C_HEADS * C_DH, C_KV_HEADS * C_DH
            wq_perm = w[:, :nq].reshape(d, C_HEADS, C_DH)[:, jnp.array(_HEAD_OF_SLOT), :].reshape(d, nq)
            win = jnp.concatenate([wq_perm, w[:, nq:nq + 2 * nk]], axis=1).astype(BF16)
            wft = jnp.transpose(w[:, nq + 2 * nk:]).astype(BF16)
            bf = b_forget[o].reshape(C_HEADS, 1)
            wout = w_out_odd[o].reshape(C_HEADS, C_DH, d)[jnp.array(_HEAD_OF_SLOT)].reshape(nq, d).astype(BF16)

            q, k, kb, v, vb, lft, ct, cn = _odd_in(hp, g0, win, wft, bf, tm_p, LOG2E)
            op = _prompt_attn(_prompt_c_kernel, [q, ct, kb, vb, cn],
                              [_q_tile_spec(q, tq), pl.BlockSpec((1, C_HEADS, tq), lambda b, i: (b, 0, i)),
                               _seq_spec(kb), _seq_spec(vb), _seq_spec(cn)],
                              bp, tp, 1024, 512, [(C_DH, 2 * tq)] * 4, tq, "prompt_attn_c")
            outs_p["ck"].append(k.reshape(bp, tp, C_KV_HEADS, C_DH))
            outs_p["cv"].append(v.reshape(bp, tp, C_KV_HEADS, C_DH))
            outs_p["cf"].append(jnp.swapaxes(lft, 1, 2))

            q, k, kb, v, vb, lft, ct, cn = _odd_in(hs, g0, win, wft, bf, tm_s, 1.0)
            qh = q.reshape(bs, ts, C_HEADS, C_DH)[:, :, jnp.array(_SLOT_OF_HEAD), :]
            qh = jnp.swapaxes(qh, 1, 2)
            selc = (jnp.arange(C_HEADS)[:, None] // 2 == jnp.arange(C_KV_HEADS)[None, :])
            q_bd = jnp.where(selc[None, :, None, :, None], qh[:, :, :, None, :], jnp.zeros((), BF16))
            q_bd = q_bd.reshape(bs, C_HEADS * ts, 512)
            k3, v3 = k.reshape(bs, ts, 512), v.reshape(bs, ts, 512)
            lf3 = jnp.swapaxes(lft.reshape(C_HEADS, bs, ts), 0, 1)
            (os_,) = _decode_attn(functools.partial(_decode_c_kernel, n_pp), page_table, [q_bd],
                                  [ckT[o:o + 1], cvT[o:o + 1], cfT[o:o + 1]],
                                  [_feature_major_page(kb.reshape(bs, ts, 512)),
                                   _feature_major_page(vb.reshape(bs, ts, 512)), _pad_axis(lf3, 2, PAGE)],
                                  [tri_ge, tri_le], [1024],
                                  _softmax_scratch(C_HEADS * 8, 512) * DECODE_SPLIT
                                  + [pltpu.VMEM((C_HEADS, 1), F32), pltpu.VMEM((C_HEADS * 8, 1), F32)],
                                  n_pp, True, "decode_attn_c")
            outs_s["ck"].append(k3.reshape(bs, ts, C_KV_HEADS, C_DH))
            outs_s["cv"].append(v3.reshape(bs, ts, C_KV_HEADS, C_DH))
            outs_s["cf"].append(jnp.swapaxes(lf3, 1, 2))

            hp = _mix_out(_odd_out_kernel, hp, (op,), [wout, g1], tm_p, "odd_out_proj")
            hs = _mix_out(_odd_out_kernel, hs, (os_.reshape(1, ns, nq),), [wout, g1], tm_s, "odd_out_proj")

        wg, wu, wd = ffn_w_gate[li].astype(BF16), ffn_w_up[li].astype(BF16), ffn_w_down[li].astype(BF16)
        cw, cb = ffn_conv_w[li], ffn_conv_b[li].reshape(1, dff)
        hp, tail = _ffn(hp, g2, wg, wu, cw, cb, wd, g3, None, tm_p, tf)
        outs_p["conv"].append(tail[:, 8 - (CONV_W - 1):])
        buf = state_conv[li]
        prev2 = _pad_axis(buf, 1, ts).reshape(1, ns, dff)
        prev1 = _pad_axis(buf[:, 1:], 1, ts).reshape(1, ns, dff)
        hs, gate = _ffn(hs, g2, wg, wu, cw, cb, wd, g3, (prev2, prev1), tm_s, tf)
        outs_s["conv"].append(gate.reshape(bs, ts, dff)[:, ts - (CONV_W - 1):])

    st = lambda xs: jnp.stack(xs)
    res = [hp, hs.reshape(bs, ts, d)]
    for key in ("ak", "av", "ckv", "ck", "cv", "cf", "conv"):
        res += [st(outs_p[key]), st(outs_s[key])]
    return tuple(res)
```

```python
import functools
import math

import jax
import jax.numpy as jnp
from jax import lax
from jax.experimental import pallas as pl
from jax.experimental.pallas import tpu as pltpu

F32 = jnp.float32
BF16 = jnp.bfloat16

EPS = 1e-6
ROPE_THETA = 500000.0
A_HEADS, A_DH, A_DV, A_ROT = 4, 64, 128, 16
B_HEADS, Q_LORA, KV_LORA, QK_NOPE, QK_ROPE, B_DV = 4, 384, 256, 128, 64, 128
C_HEADS, C_KV_HEADS, C_DH = 16, 8, 64
CONV_W = 3
PAGE = 128
LANES = 128
FLASH_COLS = 128
DECODE_SPLIT = 1
LOG2E = math.log2(math.e)
NEG = -1e30
VMEM_LIMIT_BYTES = 56 * 1024 * 1024

_NT = (((1,), (1,)), ((), ()))


def _cp(sem):
    return pltpu.CompilerParams(dimension_semantics=sem, vmem_limit_bytes=VMEM_LIMIT_BYTES)


def _rms(x, g):
    return x * lax.rsqrt(jnp.mean(x * x, axis=-1, keepdims=True) + EPS) * g


def _dot(a, b):
    return jnp.dot(a, b, preferred_element_type=F32)


def _dot_nt(a, b):
    return lax.dot_general(a, b, _NT, preferred_element_type=F32)


def _rope_chunk(x, c, sl, sr, shift):
    return x * c + pltpu.roll(x, LANES - shift, 1) * sl + pltpu.roll(x, shift, 1) * sr


def _rope_tables(pos, rot):
    half = rot // 2
    inv = jnp.power(jnp.float32(ROPE_THETA), -jnp.arange(half, dtype=F32) / half)
    ang = pos.astype(F32)[:, None] * inv[None, :]
    cos, sin = jnp.cos(ang), jnp.sin(ang)
    t = pos.shape[0]
    one = jnp.ones((t, 64 - rot), F32)
    z_rest = jnp.zeros((t, 64 - rot), F32)
    z_half = jnp.zeros((t, half), F32)
    c = jnp.concatenate([cos, cos, one], axis=1)
    sl = jnp.concatenate([-sin, z_half, z_rest], axis=1)
    sr = jnp.concatenate([z_half, sin, z_rest], axis=1)
    return tuple(jnp.tile(a, (1, 2)) for a in (c, sl, sr))


def _even_in_kernel(logit_unit, h_ref, g0_ref, win_ref, gq_ref, gkv_ref, wq_ref, wuk_ref,
                    ca_ref, sla_ref, sra_ref, cb_ref, slb_ref, srb_ref,
                    qa_ref, ka_ref, kab_ref, va_ref, vab_ref, qlat_ref, qrp_ref, ckv_ref, ckvb_ref):
    hn = _rms(h_ref[0], g0_ref[...]).astype(BF16)
    z = _dot(hn, win_ref[...])
    ca, sla, sra = ca_ref[...], sla_ref[...], sra_ref[...]
    cb, slb, srb = cb_ref[...], slb_ref[...], srb_ref[...]
    for j in range(4):
        sl = slice(LANES * j, LANES * (j + 1))
        qa_ref[0, :, sl] = (_rope_chunk(z[:, sl], ca, sla, sra, A_ROT // 2)
                            * (A_DH ** -0.5 * logit_unit)).astype(BF16)
        kr = _rope_chunk(z[:, 512 + LANES * j:512 + LANES * (j + 1)], ca, sla, sra, A_ROT // 2)
        ka_ref[0, :, sl] = kr
        kab_ref[0, :, sl] = kr.astype(BF16)
    va = z[:, 1024:1536]
    tm = va.shape[0]
    for h in range(A_HEADS):
        va_ref[0, pl.ds(h, tm, stride=A_HEADS), :] = va[:, A_DV * h:A_DV * (h + 1)]
    vab_ref[0] = va.astype(BF16)
    cqn = _rms(z[:, 1536:1920], gq_ref[...]).astype(BF16)
    q2 = _dot(cqn, wq_ref[...])
    for h in range(B_HEADS):
        qn = q2[:, LANES * h:LANES * (h + 1)].astype(BF16)
        qlat_ref[0, :, KV_LORA * h:KV_LORA * (h + 1)] = _dot(qn, wuk_ref[h]).astype(BF16)
        qr = _rope_chunk(q2[:, 512 + LANES * h:512 + LANES * (h + 1)], cb, slb, srb, QK_ROPE // 2)
        qrp_ref[0, :, LANES * h:LANES * (h + 1)] = qr.astype(BF16)
    cn = _rms(z[:, 1920:2176], gkv_ref[...])
    krr = _rope_chunk(z[:, 2176:2304], cb, slb, srb, QK_ROPE // 2)
    ckv_ref[0, :, 0:KV_LORA] = cn
    ckv_ref[0, :, KV_LORA:KV_LORA + QK_ROPE] = krr[:, :QK_ROPE]
    ckvb_ref[0, :, 0:KV_LORA] = cn.astype(BF16)
    ckvb_ref[0, :, KV_LORA:KV_LORA + LANES] = krr.astype(BF16)


def _tab_spec(tm, n_tab_tiles):
    return pl.BlockSpec((tm, LANES), lambda b, i: (i % n_tab_tiles, 0))


def _full_spec(shape):
    nd = len(shape)
    return pl.BlockSpec(shape, lambda b, i: (0,) * nd)


def _even_in(h, g0, win, gq, gkv, wq, wuk, tabs_a, tabs_b, tm, logit_unit):
    bk, tk, d = h.shape
    n_tab = tabs_a[0].shape[0] // tm
    tok = lambda w: pl.BlockSpec((1, tm, w), lambda b, i: (b, i, 0))
    out_shapes = [
        jax.ShapeDtypeStruct((bk, tk, 512), BF16),
        jax.ShapeDtypeStruct((bk, tk, 512), F32),
        jax.ShapeDtypeStruct((bk, tk, 512), BF16),
        jax.ShapeDtypeStruct((bk, tk * A_HEADS, A_DV), F32),
        jax.ShapeDtypeStruct((bk, tk, 512), BF16),
        jax.ShapeDtypeStruct((bk, tk, 1024), BF16),
        jax.ShapeDtypeStruct((bk, tk, 512), BF16),
        jax.ShapeDtypeStruct((bk, tk, 320), F32),
        jax.ShapeDtypeStruct((bk, tk, 384), BF16),
    ]
    return pl.pallas_call(
        functools.partial(_even_in_kernel, logit_unit),
        grid=(bk, tk // tm),
        in_specs=[tok(d), _full_spec(g0.shape), _full_spec(win.shape), _full_spec(gq.shape),
                  _full_spec(gkv.shape), _full_spec(wq.shape), _full_spec(wuk.shape)]
                 + [_tab_spec(tm, n_tab)] * 6,
        out_specs=[pl.BlockSpec((1, tm * (s.shape[1] // tk), s.shape[2]), lambda b, i: (b, i, 0))
                   for s in out_shapes],
        out_shape=out_shapes,
        compiler_params=_cp(("parallel", "parallel")),
        name="even_in_proj",
    )(h, g0, win, gq, gkv, wq, wuk, *tabs_a, *tabs_b)


def _split3(x):
    hi = x.astype(BF16)
    r = x - hi.astype(F32)
    mid = r.astype(BF16)
    lo = (r - mid.astype(F32)).astype(BF16)
    return hi, mid, lo


def _tri_dot(x, tri):
    n = x.shape[0]
    hi, mid, lo = _split3(x)
    y = _dot(jnp.concatenate([hi, mid, lo], axis=0), tri)
    return y[0:n] + y[n:2 * n] + y[2 * n:3 * n]


def _log_sigmoid(x):
    return -(jnp.maximum(-x, 0.0) + jnp.log1p(jnp.exp(-jnp.abs(x))))


def _odd_in_kernel(logit_unit, h_ref, g0_ref, win_ref, wft_ref, bf_ref, tri_ref,
                   q_ref, k_ref, kb_ref, v_ref, vb_ref, lft_ref, ct_ref, cn_ref, carry_ref):
    i = pl.program_id(1)
    hn = _rms(h_ref[0], g0_ref[...]).astype(BF16)
    z = _dot(hn, win_ref[...])
    q_ref[0] = (z[:, 0:1024] * (C_DH ** -0.5 * logit_unit)).astype(BF16)
    k = z[:, 1024:1536]
    k_ref[0] = k
    kb_ref[0] = k.astype(BF16)
    v = z[:, 1536:2048]
    v_ref[0] = v
    vb_ref[0] = v.astype(BF16)
    ft = _dot_nt(wft_ref[...], hn)
    lft = _log_sigmoid(ft + bf_ref[...])
    lft_ref[0] = lft

    @pl.when(i == 0)
    def _():
        carry_ref[...] = jnp.zeros_like(carry_ref)

    c = _tri_dot(lft, tri_ref[...]) + carry_ref[...]
    carry_ref[...] = c[:, -1:]
    c = c * logit_unit
    ct_ref[0] = c
    c128 = jnp.concatenate([c, jnp.zeros((LANES - C_HEADS, c.shape[1]), F32)], axis=0)
    cn_ref[0] = jnp.transpose(c128)[:, 0:C_HEADS]


def _odd_in(h, g0, win, wft, bf, tm, logit_unit):
    bk, tk, d = h.shape
    tri = jnp.triu(jnp.ones((tm, tm), F32)).astype(BF16)
    tok = lambda w: pl.BlockSpec((1, tm, w), lambda b, i: (b, i, 0))
    tokt = pl.BlockSpec((1, C_HEADS, tm), lambda b, i: (b, 0, i))
    out_shapes = [
        jax.ShapeDtypeStruct((bk, tk, 1024), BF16),
        jax.ShapeDtypeStruct((bk, tk, 512), F32),
        jax.ShapeDtypeStruct((bk, tk, 512), BF16),
        jax.ShapeDtypeStruct((bk, tk, 512), F32),
        jax.ShapeDtypeStruct((bk, tk, 512), BF16),
        jax.ShapeDtypeStruct((bk, C_HEADS, tk), F32),
        jax.ShapeDtypeStruct((bk, C_HEADS, tk), F32),
        jax.ShapeDtypeStruct((bk, tk, C_HEADS), F32),
    ]
    return pl.pallas_call(
        functools.partial(_odd_in_kernel, logit_unit),
        grid=(bk, tk // tm),
        in_specs=[tok(d), _full_spec(g0.shape), _full_spec(win.shape), _full_spec(wft.shape),
                  _full_spec(bf.shape), _full_spec(tri.shape)],
        out_specs=[tok(1024), tok(512), tok(512), tok(512), tok(512), tokt, tokt, tok(C_HEADS)],
        out_shape=out_shapes,
        scratch_shapes=[pltpu.VMEM((C_HEADS, 1), F32)],
        compiler_params=_cp(("parallel", "arbitrary")),
        name="odd_in_proj",
    )(h, g0, win, wft, bf, tri)


def _even_out_kernel(h_ref, oa_ref, olat_ref, wuv_ref, wout_ref, g1_ref, out_ref):
    parts = [oa_ref[0]]
    for hd in range(B_HEADS):
        parts.append(_dot(olat_ref[0, :, KV_LORA * hd:KV_LORA * (hd + 1)], wuv_ref[hd]).astype(BF16))
    y = _dot(jnp.concatenate(parts, axis=1), wout_ref[...])
    out_ref[0] = h_ref[0] + _rms(y, g1_ref[...])


def _odd_out_kernel(h_ref, o_ref, wout_ref, g1_ref, out_ref):
    y = _dot(o_ref[0], wout_ref[...])
    out_ref[0] = h_ref[0] + _rms(y, g1_ref[...])


def _mix_out(kern, h, acts, consts, tm, name):
    bk, tk, d = h.shape
    tok = lambda w: pl.BlockSpec((1, tm, w), lambda b, i: (b, i, 0))
    return pl.pallas_call(
        kern,
        grid=(bk, tk // tm),
        in_specs=[tok(d)] + [tok(a.shape[-1]) for a in acts] + [_full_spec(c.shape) for c in consts],
        out_specs=tok(d),
        out_shape=jax.ShapeDtypeStruct(h.shape, F32),
        compiler_params=_cp(("parallel", "parallel")),
        name=name,
    )(h, *acts, *consts)


def _ffn_kernel(short_seq, *refs):
    if short_seq:
        (h_ref, g2_ref, wg_ref, wu_ref, cw_ref, cb_ref, wd_ref, g3_ref, pa_ref, pb_ref,
         out_ref, gate_ref, hn_ref, acc_ref) = refs
    else:
        (h_ref, g2_ref, wg_ref, wu_ref, cw_ref, cb_ref, wd_ref, g3_ref,
         out_ref, tail_ref, hn_ref, acc_ref, carry_ref) = refs
    i = pl.program_id(1)
    c = pl.program_id(2)

    @pl.when(c == 0)
    def _():
        hn_ref[...] = _rms(h_ref[0], g2_ref[...]).astype(BF16)
        acc_ref[...] = jnp.zeros_like(acc_ref)

    hn = hn_ref[...]
    g = _dot(hn, wg_ref[...])
    u = _dot(hn, wu_ref[...])
    tm = g.shape[0]
    row = lax.broadcasted_iota(jnp.int32, g.shape, 0)
    r1 = pltpu.roll(g, 1, 0)
    r2 = pltpu.roll(g, 2, 0)
    if short_seq:
        t = row % 8
        gm1 = jnp.where(t == 0, pb_ref[0], r1)
        gm2 = jnp.where(t < 2, pa_ref[0], r2)
        gate_ref[0] = g
    else:
        top = jnp.where(i == 0, 0.0, carry_ref[c])
        gm1 = jnp.where(row == 0, top[7:8], r1)
        gm2 = jnp.where(row == 0, top[6:7], jnp.where(row == 1, top[7:8], r2))
        carry_ref[c] = g[tm - 8:tm]
        tail_ref[0] = g[tm - 8:tm]
    cw = cw_ref[...]
    gc = cb_ref[...] + ((cw[0:1] * gm2 + cw[1:2] * gm1) + cw[2:3] * g)
    act = (jax.nn.silu(gc) * u).astype(BF16)
    acc_ref[...] += _dot(act, wd_ref[...])

    @pl.when(c == pl.num_programs(2) - 1)
    def _():
        out_ref[0] = h_ref[0] + _rms(acc_ref[...], g3_ref[...])


def _ffn(h, g2, wg, wu, cw, cb, wd, g3, prev, tm, tf):
    bk, tk, d = h.shape
    dff = wg.shape[1]
    nc = dff // tf
    short_seq = prev is not None
    tok = pl.BlockSpec((1, tm, d), lambda b, i, c: (b, i, 0))
    const = lambda shape: pl.BlockSpec(shape, lambda b, i, c: (0,) * len(shape))
    in_specs = [tok, const(g2.shape),
                pl.BlockSpec((d, tf), lambda b, i, c: (0, c)),
                pl.BlockSpec((d, tf), lambda b, i, c: (0, c)),
                pl.BlockSpec((CONV_W, tf), lambda b, i, c: (0, c)),
                pl.BlockSpec((1, tf), lambda b, i, c: (0, c)),
                pl.BlockSpec((tf, d), lambda b, i, c: (c, 0)),
                const(g3.shape)]
    args = [h, g2, wg, wu, cw, cb, wd, g3]
    scratch = [pltpu.VMEM((tm, d), BF16), pltpu.VMEM((tm, d), F32)]
    if short_seq:
        in_specs += [pl.BlockSpec((1, tm, tf), lambda b, i, c: (b, i, c))] * 2
        args += list(prev)
        out_specs = [tok, pl.BlockSpec((1, tm, tf), lambda b, i, c: (b, i, c))]
        out_shape = [jax.ShapeDtypeStruct(h.shape, F32), jax.ShapeDtypeStruct((bk, tk, dff), F32)]
    else:
        out_specs = [tok, pl.BlockSpec((1, 8, tf), lambda b, i, c: (b, 0, c))]
        out_shape = [jax.ShapeDtypeStruct(h.shape, F32), jax.ShapeDtypeStruct((bk, 8, dff), F32)]
        scratch.append(pltpu.VMEM((nc, 8, tf), F32))
    return pl.pallas_call(
        functools.partial(_ffn_kernel, short_seq),
        grid=(bk, tk // tm, nc),
        in_specs=in_specs,
        out_specs=out_specs,
        out_shape=out_shape,
        scratch_shapes=scratch,
        compiler_params=_cp(("parallel", "arbitrary", "arbitrary")),
        name="conv_ffn",
    )(*args)


def _flash_cols(i, tq, chains):
    subs = []
    for c in chains:
        c["acc"][...] = jnp.zeros_like(c["acc"])
        for c0 in range(0, c["acc"].shape[1], c["cw"]):
            subs.append((c, slice(c0, c0 + c["cw"])))

    def step(j, stats, masked):
        kv = pl.ds(pl.multiple_of(j * tq, tq), tq)
        new_stats = []
        for (c, cols), (m_prev, l_prev) in zip(subs, stats):
            s = c["qk"](kv, cols)
            if c["scale"] is not None:
                s = s * c["scale"]
            if c["bias"] is not None:
                s = s + c["bias"](kv, cols)
            if masked:
                key = lax.broadcasted_iota(jnp.int32, s.shape, 0)
                qpos = (lax.broadcasted_iota(jnp.int32, s.shape, 1) + cols.start) % tq
                s = jnp.where(key <= qpos, s, NEG)
            m_new = jnp.maximum(m_prev, jnp.max(s, axis=0, keepdims=True))
            alpha = jnp.exp2(m_prev - m_new)
            p = jnp.exp2(s - m_new)
            c["acc"][:, cols] = alpha * c["acc"][:, cols] + _dot(c["vt"](kv), p.astype(BF16))
            new_stats.append((m_new, alpha * l_prev + jnp.sum(p, axis=0, keepdims=True)))
        return tuple(new_stats)

    init = tuple((jnp.full((1, c["cw"]), NEG, F32), jnp.zeros((1, c["cw"]), F32)) for c, _ in subs)
    stats = lax.fori_loop(0, i, lambda j, st: step(j, st, False), init)
    stats = step(i, stats, True)
    outs, k = [], 0
    for c in chains:
        n = c["acc"].shape[1] // c["cw"]
        l = jnp.concatenate([stats[k + t][1] for t in range(n)], axis=1)
        outs.append(c["acc"][...] / l)
        k += n
    return outs


def _transpose_bf16(x):
    return jnp.transpose(x.astype(F32)).astype(BF16)


def _fill_transposed(i, tq, src_fn, dst_ref):
    n_tiles = dst_ref.shape[1] // tq

    @pl.when(i == 0)
    def _():
        def body(c, carry):
            rows = pl.ds(pl.multiple_of(c * tq, tq), tq)
            dst_ref[:, rows] = _transpose_bf16(src_fn(rows))
            return carry

        lax.fori_loop(0, n_tiles, body, 0)


def _chain_scratch(dv, r):
    return [pltpu.VMEM((dv, r), F32)]


def _lambda_value(lam_ref, lam_init):
    lam = lam_ref[...]
    s01 = jnp.sum(lam[0:1] * lam[1:2], axis=-1, keepdims=True)
    s23 = jnp.sum(lam[2:3] * lam[3:4], axis=-1, keepdims=True)
    return jnp.exp(s01) - jnp.exp(s23) + lam_init


def _half_rows(blk, half):
    z = jnp.zeros_like(blk)
    return jnp.concatenate([blk, z] if half == 0 else [z, blk], axis=0)


def _prompt_a_kernel(lam_init, q_ref, k_ref, v_ref, lam_ref, gsub_ref, out_ref, vt_ref, *st):
    i = pl.program_id(1)
    tq = q_ref.shape[1]
    _fill_transposed(i, tq, lambda rows: v_ref[0, rows, :], vt_ref)
    qt = _transpose_bf16(q_ref[0])
    lam_val = _lambda_value(lam_ref, lam_init)
    for heads in (tuple(range(A_HEADS)),):
        chains = []
        for hd in heads:
            slots = [s * A_HEADS + hd for s in range(2)]
            qts = [_half_rows(qt[64 * e:64 * (e + 1)], e % 2) for e in slots]
            ksl = [slice(LANES * (e // 2), LANES * (e // 2 + 1)) for e in slots]

            def qk(kv, cols, qts=qts, ksl=ksl):
                s, lo = divmod(cols.start, tq)
                return _dot(k_ref[0, kv, ksl[s]], qts[s][:, lo:lo + cols.stop - cols.start])

            def vt(kv, hd=hd):
                return vt_ref[A_DV * hd:A_DV * (hd + 1), kv]

            chains.append(dict(qk=qk, vt=vt, scale=None, bias=None, acc=st[hd], cw=FLASH_COLS))
        for hd, ot in zip(heads, _flash_cols(i, tq, chains)):
            d = jnp.transpose(ot[:, 0:tq] - lam_val * ot[:, tq:2 * tq])
            out_ref[0, :, A_DV * hd:A_DV * (hd + 1)] = (_rms(d, gsub_ref[...]) * (1.0 - lam_init)).astype(BF16)


def _prompt_b_kernel(q1_ref, q2_ref, c_ref, out_ref, vt_ref, acc_ref):
    i = pl.program_id(1)
    tq = q1_ref.shape[1]
    _fill_transposed(i, tq, lambda rows: c_ref[0, rows, 0:KV_LORA], vt_ref)
    q1t = _transpose_bf16(q1_ref[0])
    q2t = _transpose_bf16(q2_ref[0])
    q1s = jnp.concatenate([q1t[KV_LORA * hd:KV_LORA * (hd + 1)] for hd in range(B_HEADS)], axis=1)
    q2s = jnp.concatenate([q2t[LANES * hd:LANES * (hd + 1)] for hd in range(B_HEADS)], axis=1)

    def qk(kv, cols):
        return (_dot(c_ref[0, kv, 0:KV_LORA], q1s[:, cols])
                + _dot(c_ref[0, kv, KV_LORA:KV_LORA + LANES], q2s[:, cols]))

    chain = dict(qk=qk, vt=lambda kv: vt_ref[:, kv], scale=(QK_NOPE + QK_ROPE) ** -0.5 * LOG2E, bias=None,
                 acc=acc_ref,
                 cw=B_HEADS * tq)
    (ot,) = _flash_cols(i, tq, [chain])
    for hd in range(B_HEADS):
        out_ref[0, :, KV_LORA * hd:KV_LORA * (hd + 1)] = jnp.transpose(ot[:, tq * hd:tq * (hd + 1)]).astype(BF16)


def _prompt_c_kernel(q_ref, ct_ref, k_ref, v_ref, cn_ref, out_ref, vt_ref, *st):
    i = pl.program_id(1)
    tq = q_ref.shape[1]
    _fill_transposed(i, tq, lambda rows: v_ref[0, rows, :], vt_ref)
    qt = _transpose_bf16(q_ref[0])
    n_pairs = len(st) // 2
    for pp0 in range(0, C_KV_HEADS // 2, n_pairs):
        chains = []
        for pp in range(pp0, pp0 + n_pairs):
            for half in range(2):
                heads = [4 * pp + 2 * half + r for r in range(2)]
                blk = jnp.concatenate(
                    [qt[LANES * (2 * pp + r) + 64 * half:LANES * (2 * pp + r) + 64 * (half + 1)]
                     for r in range(2)], axis=1)
                qs = _half_rows(blk, half)
                cq = [ct_ref[0, hd:hd + 1, :] for hd in heads]

                def bias(kv, cols, heads=heads, cq=cq):
                    return jnp.concatenate([cq[r] - cn_ref[0, kv, hd:hd + 1] for r, hd in enumerate(heads)],
                                           axis=1)

                def qk(kv, cols, qs=qs, pp=pp):
                    return _dot(k_ref[0, kv, LANES * pp:LANES * (pp + 1)], qs[:, cols])

                def vt(kv, g=2 * pp + half):
                    return vt_ref[C_DH * g:C_DH * (g + 1), kv]

                chains.append(dict(qk=qk, vt=vt, scale=None, bias=bias, acc=st[2 * (pp - pp0) + half],
                                   cw=2 * tq))
        outs = _flash_cols(i, tq, chains)
        for pp in range(pp0, pp0 + n_pairs):
            ot = jnp.concatenate(outs[2 * (pp - pp0):2 * (pp - pp0) + 2], axis=0)
            for r in range(2):
                out_ref[0, :, LANES * (2 * pp + r):LANES * (2 * pp + r + 1)] = jnp.transpose(
                    ot[:, tq * r:tq * (r + 1)]).astype(BF16)


def _prompt_attn(kern, args, in_specs, bk, t, out_w, vt_rows, chain_shapes, tq, name):
    scratch = [pltpu.VMEM((vt_rows, t), BF16)]
    for dv, r in chain_shapes:
        scratch += _chain_scratch(dv, r)
    return pl.pallas_call(
        kern,
        grid=(bk, t // tq),
        in_specs=in_specs,
        out_specs=pl.BlockSpec((1, tq, out_w), lambda b, i: (b, i, 0)),
        out_shape=jax.ShapeDtypeStruct((bk, t, out_w), BF16),
        scratch_shapes=scratch,
        compiler_params=_cp(("parallel", "arbitrary")),
        name=name,
    )(*args)


def _q_tile_spec(a, tq):
    return pl.BlockSpec((1, tq, a.shape[-1]), lambda b, i: (b, i, 0))


def _seq_spec(a):
    return pl.BlockSpec((1,) + a.shape[1:], lambda b, i: (b, 0, 0))


def _online_update(s_list, pv_fn, m_ref, l_ref, acc_ref):
    m_prev = m_ref[...]
    m_new = m_prev
    for s in s_list:
        m_new = jnp.maximum(m_new, jnp.max(s, axis=-1, keepdims=True))
    alpha = jnp.exp(m_prev - m_new)
    ps = [jnp.exp(s - m_new) for s in s_list]
    l_new = alpha * l_ref[...]
    for p in ps:
        l_new = l_new + jnp.sum(p, axis=-1, keepdims=True)
    l_ref[...] = l_new
    acc_ref[...] = alpha * acc_ref[...] + pv_fn([p.astype(BF16) for p in ps])
    m_ref[...] = m_new


def _page_groups(n_pp):
    per = -(-n_pp // DECODE_SPLIT)
    return [list(range(g * per, min((g + 1) * per, n_pp))) for g in range(DECODE_SPLIT)]


def _merged_softmax(states):
    ms = [m[...] for m, _, _ in states]
    m_all = functools.reduce(jnp.maximum, ms)
    num, den = None, None
    for m, (_, l_ref, acc_ref) in zip(ms, states):
        w = jnp.exp(m - m_all)
        num = acc_ref[...] * w if num is None else num + acc_ref[...] * w
        den = l_ref[...] * w if den is None else den + l_ref[...] * w
    return num / den


def _init_softmax(m_ref, l_ref, acc_ref):
    m_ref[...] = jnp.full_like(m_ref, NEG)
    l_ref[...] = jnp.zeros_like(l_ref)
    acc_ref[...] = jnp.zeros_like(acc_ref)


def _new_token_mask(s):
    row = lax.broadcasted_iota(jnp.int32, s.shape, 0) % 8
    col = lax.broadcasted_iota(jnp.int32, s.shape, 1)
    return jnp.where(col <= row, s, NEG)


def _decode_ab_kernel(n_pp, lam_init, pt_ref, qa_ref, qb_ref, *refs):
    ak_refs, av_refs, bc_refs = refs[:n_pp], refs[n_pp:2 * n_pp], refs[2 * n_pp:3 * n_pp]
    kn_ref, vn_ref, cn_ref, lam_ref, gsub_ref, oa_ref, ob_ref = refs[3 * n_pp:3 * n_pp + 7]
    st = refs[3 * n_pp + 7:]
    sa = [st[3 * g:3 * g + 3] for g in range(DECODE_SPLIT)]
    sb = [st[3 * (DECODE_SPLIT + g):3 * (DECODE_SPLIT + g) + 3] for g in range(DECODE_SPLIT)]
    j = pl.program_id(1)
    qa = qa_ref[0]
    qb = qb_ref[0]
    scale_b = (QK_NOPE + QK_ROPE) ** -0.5

    def update_a(state, k_pages, v_of, mask):
        s_list = [_dot(qa, k) for k in k_pages]
        if mask:
            s_list = [_new_token_mask(s) for s in s_list]

        def pv(ps):
            outs = []
            for hd in range(A_HEADS):
                o = None
                for p, pr in enumerate(ps):
                    t = _dot(pr[16 * hd:16 * (hd + 1)], v_of(p, hd))
                    o = t if o is None else o + t
                outs.append(o)
            return jnp.concatenate(outs, axis=0)

        _online_update(s_list, pv, *state)

    def update_b(state, pages, mask):
        s_list = [_dot(qb, c) * scale_b for c in pages]
        if mask:
            s_list = [_new_token_mask(s) for s in s_list]

        def pv(ps):
            o = None
            for pr, c in zip(ps, pages):
                t = _dot_nt(pr, c[0:KV_LORA])
                o = t if o is None else o + t
            return o

        _online_update(s_list, pv, *state)

    @pl.when(j == 0)
    def _():
        for state in sa + sb:
            _init_softmax(*state)

    for g, grp in enumerate(_page_groups(n_pp)):
        if not grp:
            continue
        update_a(sa[g], [ak_refs[p][0, 0].astype(BF16) for p in grp],
                 lambda p, hd, grp=grp: av_refs[grp[p]][0, 0, pl.ds(hd, PAGE, stride=A_HEADS), :].astype(BF16),
                 False)
        update_b(sb[g], [bc_refs[p][0, 0].astype(BF16) for p in grp], False)

    @pl.when(j == pl.num_programs(1) - 1)
    def _():
        update_a(sa[0], [kn_ref[0].astype(BF16)],
                 lambda p, hd: vn_ref[0, pl.ds(hd, PAGE, stride=A_HEADS), :].astype(BF16), True)
        update_b(sb[0], [cn_ref[0].astype(BF16)], True)
        o = _merged_softmax(sa)
        lam_val = _lambda_value(lam_ref, lam_init)
        for hd in range(A_HEADS):
            d = o[16 * hd:16 * hd + 8] - lam_val * o[16 * hd + 8:16 * hd + 16]
            oa_ref[0, :, A_DV * hd:A_DV * (hd + 1)] = (
                _rms(d, gsub_ref[...]) * (1.0 - lam_init)).astype(BF16)
        o = _merged_softmax(sb)
        for hd in range(B_HEADS):
            ob_ref[0, :, KV_LORA * hd:KV_LORA * (hd + 1)] = o[8 * hd:8 * (hd + 1)].astype(BF16)


def _expand_heads(x):
    return jnp.broadcast_to(x[:, None, :], (C_HEADS, 8, x.shape[-1])).reshape(C_HEADS * 8, x.shape[-1])


def _decode_c_kernel(n_pp, pt_ref, q_ref, *refs):
    k_refs, v_refs, lf_refs = refs[:n_pp], refs[n_pp:2 * n_pp], refs[2 * n_pp:3 * n_pp]
    kn_ref, vn_ref, lfn_ref, tri_ge_ref, tri_le_ref, out_ref = refs[3 * n_pp:3 * n_pp + 6]
    st = refs[3 * n_pp + 6:]
    states = [st[3 * g:3 * g + 3] for g in range(DECODE_SPLIT)]
    base_ref, bq_ref = st[3 * DECODE_SPLIT:]
    j = pl.program_id(1)
    last = pl.num_programs(1) - 1
    q = q_ref[0]

    def scores(k_page):
        return _dot(q, k_page)

    def pv_of(pages):
        def pv(ps):
            o = None
            for pr, v in zip(ps, pages):
                t = _dot_nt(pr, v)
                o = t if o is None else o + t
            return o
        return pv

    @pl.when(j == 0)
    def _():
        for state in states:
            _init_softmax(*state)
        base_ref[...] = jnp.zeros_like(base_ref)
        cnl = _tri_dot(lfn_ref[0], tri_le_ref[...])
        rep = _expand_heads(cnl)
        row = lax.broadcasted_iota(jnp.int32, rep.shape, 0) % 8
        col = lax.broadcasted_iota(jnp.int32, rep.shape, 1)
        bq = jnp.sum(jnp.where(col == row, rep, 0.0), axis=-1, keepdims=True)
        bq_ref[...] = bq
        s_new = _new_token_mask(scores(kn_ref[0].astype(BF16)) + (bq - rep))
        _online_update([s_new], pv_of([vn_ref[0].astype(BF16)]), *states[0])

    lf = jnp.concatenate([lf_refs[p][0, 0] for p in range(n_pp)], axis=0)
    incl = _tri_dot(lf, tri_ge_ref[...])
    excl = incl - lf
    bq = bq_ref[...]
    base = base_ref[...]
    s_list = []
    for p in range(n_pp):
        sl = slice(C_HEADS * p, C_HEADS * (p + 1))
        bias = _expand_heads(base + excl[sl])
        s_list.append(scores(k_refs[p][0, 0].astype(BF16)) + (bq + bias))
        base = base + incl[sl, 0:1]
    base_ref[...] = base
    for g, grp in enumerate(_page_groups(n_pp)):
        if grp:
            _online_update([s_list[p] for p in grp], pv_of([v_refs[p][0, 0].astype(BF16) for p in grp]),
                           *states[g])

    @pl.when(j == last)
    def _():
        o = _merged_softmax(states)
        lane = lax.broadcasted_iota(jnp.int32, (8, LANES), 1)
        for pp in range(C_KV_HEADS // 2):
            csl = slice(LANES * pp, LANES * (pp + 1))
            for r in range(2):
                h0, h1 = 4 * pp + r, 4 * pp + 2 + r
                chunk = jnp.where(lane < 64, o[8 * h0:8 * (h0 + 1), csl], o[8 * h1:8 * (h1 + 1), csl])
                out_ref[0, :, LANES * (2 * pp + r):LANES * (2 * pp + r + 1)] = chunk.astype(BF16)


def _softmax_scratch(rows, dv):
    return [pltpu.VMEM((rows, 1), F32), pltpu.VMEM((rows, 1), F32), pltpu.VMEM((rows, dv), F32)]


def _decode_attn(kern, page_table, qs, caches, news, consts, out_ws, scratch, n_pp, reverse, name):
    nb, n_pages = page_table.shape
    n_steps = n_pages // n_pp

    def page_spec(c, p):
        def imap(b, j, pt):
            idx = j * n_pp + p
            if reverse:
                idx = n_pages - 1 - idx
            return (0, pt[b, idx], 0, 0)
        return pl.BlockSpec((1, 1) + c.shape[2:], imap)

    in_specs, args = [], []
    for q in qs:
        in_specs.append(pl.BlockSpec((1,) + q.shape[1:], lambda b, j, pt: (b, 0, 0)))
        args.append(q)
    for c in caches:
        for p in range(n_pp):
            in_specs.append(page_spec(c, p))
            args.append(c)
    for a in news:
        in_specs.append(pl.BlockSpec((1,) + a.shape[1:], lambda b, j, pt: (b, 0, 0)))
        args.append(a)
    for a in consts:
        in_specs.append(pl.BlockSpec(a.shape, lambda b, j, pt, nd=a.ndim: (0,) * nd))
        args.append(a)
    grid_spec = pltpu.PrefetchScalarGridSpec(
        num_scalar_prefetch=1,
        grid=(nb, n_steps),
        in_specs=in_specs,
        out_specs=[pl.BlockSpec((1, 8, w), lambda b, j, pt: (b, 0, 0)) for w in out_ws],
        scratch_shapes=scratch,
    )
    return pl.pallas_call(
        kern,
        grid_spec=grid_spec,
        out_shape=[jax.ShapeDtypeStruct((nb, 8, w), BF16) for w in out_ws],
        compiler_params=_cp(("parallel", "arbitrary")),
        name=name,
    )(page_table, *args)


_HEAD_OF_SLOT = [4 * (c // 2) + 2 * half + (c % 2) for c in range(C_HEADS // 2) for half in range(2)]
_SLOT_OF_HEAD = [_HEAD_OF_SLOT.index(h) for h in range(C_HEADS)]


def _lambda_init(layer):
    return 0.8 - 0.6 * math.exp(-0.3 * layer)


def _pad_axis(x, axis, size):
    pad = [(0, 0)] * x.ndim
    pad[axis] = (0, size - x.shape[axis])
    return jnp.pad(x, pad)


def _tok_tile(n, pref):
    tm = min(n, pref)
    assert n % tm == 0 and tm % 8 == 0
    return tm


def _feature_major_page(x):
    return _pad_axis(jnp.swapaxes(x, 1, 2), 2, PAGE)


def kernel(x_prompt, x_sample, cache_a_k, cache_a_v, cache_b_ckv, cache_c_k, cache_c_v, cache_c_logf, state_conv, page_table, norm_gains, w_in_even, w_q_up, w_kv_uk, w_kv_uv, g_q_lat, g_kv_lat, diff_lambda, g_diff_subln, w_out_even, w_in_odd, b_forget, w_out_odd, ffn_w_gate, ffn_w_up, ffn_conv_w, ffn_conv_b, ffn_w_down):
    depth = norm_gains.shape[0]
    bp, tp, d = x_prompt.shape
    bs, ts, _ = x_sample.shape
    n_pages = page_table.shape[1]
    n_pool = cache_a_k.shape[1]
    dff = ffn_w_gate.shape[2]
    assert ts == 8 and cache_a_k.shape[2] == PAGE
    ns = bs * ts
    tm_p, tm_s = _tok_tile(tp, 512), _tok_tile(ns, 512)
    tq = _tok_tile(tp, 256)
    tf = dff // 2 if dff % (2 * LANES) == 0 else dff
    n_pp = 32 if n_pages % 32 == 0 else n_pages
    assert dff % tf == 0

    pos_p = jnp.arange(tp, dtype=jnp.int32)
    pos_s = n_pages * PAGE + jnp.arange(ts, dtype=jnp.int32)
    tabs_p = (_rope_tables(pos_p, A_ROT), _rope_tables(pos_p, QK_ROPE))
    tabs_s = tuple(tuple(jnp.tile(t, (tm_s // ts, 1)) for t in tabs)
                   for tabs in (_rope_tables(pos_s, A_ROT), _rope_tables(pos_s, QK_ROPE)))

    akT = jnp.transpose(cache_a_k, (0, 1, 3, 4, 5, 2)).reshape(-1, n_pool, 512, PAGE)
    av4 = cache_a_v.reshape(-1, n_pool, PAGE * A_HEADS, A_DV)
    bcT = jnp.swapaxes(cache_b_ckv, 2, 3)
    ckT = jnp.transpose(cache_c_k, (0, 1, 3, 4, 2)).reshape(-1, n_pool, 512, PAGE)
    cvT = jnp.transpose(cache_c_v, (0, 1, 3, 4, 2)).reshape(-1, n_pool, 512, PAGE)
    cfT = jnp.swapaxes(cache_c_logf, 2, 3)

    tri_ge = jnp.tril(jnp.ones((PAGE, PAGE), F32)).astype(BF16)
    tri_le = jnp.triu(jnp.ones((PAGE, PAGE), F32)).astype(BF16)

    hp = x_prompt
    hs = x_sample.reshape(1, ns, d)
    outs_p = {k: [] for k in ("ak", "av", "ckv", "ck", "cv", "cf", "conv")}
    outs_s = {k: [] for k in outs_p}

    for li in range(depth):
        gn = norm_gains[li]
        g0, g1, g2, g3 = (gn[k].reshape(1, d) for k in range(4))
        if li % 2 == 0:
            e = li // 2
            lam_init = _lambda_init(li)
            win = _pad_axis(w_in_even[e], 1, 2304).astype(BF16)
            wq_n = w_q_up[e][:, :, :QK_NOPE].reshape(Q_LORA, B_HEADS * QK_NOPE)
            wq_r = _pad_axis(w_q_up[e][:, :, QK_NOPE:], 2, LANES).reshape(Q_LORA, B_HEADS * LANES)
            wq = jnp.concatenate([wq_n, wq_r], axis=1).astype(BF16)
            wuk = jnp.transpose(w_kv_uk[e], (1, 2, 0)).astype(BF16)
            wuv = jnp.transpose(w_kv_uv[e], (1, 0, 2)).astype(BF16)
            wout = w_out_even[e].astype(BF16)
            gq, gkv = g_q_lat[e].reshape(1, -1), g_kv_lat[e].reshape(1, -1)
            lam, gsub = diff_lambda[e], g_diff_subln[e].reshape(1, -1)

            def even_in(h, tabs, tm, logit_unit):
                return _even_in(h, g0, win, gq, gkv, wq, wuk, tabs[0], tabs[1], tm, logit_unit)

            qa, ka, kab, va, vab, qlat, qrp, ckv, ckvb = even_in(hp, tabs_p, tm_p, LOG2E)
            oa = _prompt_attn(functools.partial(_prompt_a_kernel, lam_init), [qa, kab, vab, lam, gsub],
                              [_q_tile_spec(qa, tq), _seq_spec(kab), _seq_spec(vab), _full_spec(lam.shape),
                               _full_spec(gsub.shape)],
                              bp, tp, 512, 512, [(A_DV, 2 * tq)] * A_HEADS, tq, "prompt_attn_a")
            olat = _prompt_attn(_prompt_b_kernel, [qlat, qrp, ckvb],
                                [_q_tile_spec(qlat, tq), _q_tile_spec(qrp, tq), _seq_spec(ckvb)],
                                bp, tp, 1024, KV_LORA, [(KV_LORA, B_HEADS * tq)], tq, "prompt_attn_b")
            mix_p = (oa, olat)
            outs_p["ak"].append(ka.reshape(bp, tp, 2, A_HEADS, A_DH))
            outs_p["av"].append(va.reshape(bp, tp, A_HEADS, A_DV))
            outs_p["ckv"].append(ckv)

            qa, ka, kab, va, vab, qlat, qrp, ckv, ckvb = even_in(hs, tabs_s, tm_s, 1.0)
            sel = (jnp.arange(8)[:, None] == jnp.arange(8).reshape(2, 4).T.reshape(8)[None, :])
            qx = jnp.transpose(qa.reshape(bs, ts, 2, A_HEADS, A_DH), (0, 3, 2, 1, 4)).reshape(bs, 8, ts, A_DH)
            q_bd = jnp.where(sel.T[None, :, None, :, None], qx[:, :, :, None, :], jnp.zeros((), BF16))
            q_bd = q_bd.reshape(bs, 8 * ts, 512)
            ckv3 = ckv.reshape(bs, ts, 320)
            kn = _feature_major_page(kab.reshape(bs, ts, 512))
            vn = _pad_axis(va.reshape(bs, ts * A_HEADS, A_DV), 1, PAGE * A_HEADS)
            cn_page = _feature_major_page(ckvb.reshape(bs, ts, KV_LORA + LANES)[:, :, :KV_LORA + QK_ROPE])
            q_abs = jnp.concatenate([qlat.reshape(bs, ts, B_HEADS, KV_LORA),
                                     qrp.reshape(bs, ts, B_HEADS, LANES)[..., :QK_ROPE]], axis=-1)
            q_abs = jnp.swapaxes(q_abs, 1, 2).reshape(bs, B_HEADS * ts, KV_LORA + QK_ROPE)
            oa, olat = _decode_attn(functools.partial(_decode_ab_kernel, n_pp, lam_init), page_table, [q_bd, q_abs],
                                    [akT[e:e + 1], av4[e:e + 1], bcT[e:e + 1]],
                                    [kn, vn, cn_page], [lam, gsub], [512, 1024],
                                    _softmax_scratch(64, A_DV) * DECODE_SPLIT
                                    + _softmax_scratch(32, KV_LORA) * DECODE_SPLIT, n_pp, False,
                                    "decode_attn_ab")
            mix_s = (oa.reshape(1, ns, 512), olat.reshape(1, ns, 1024))
            outs_s["ak"].append(ka.reshape(bs, ts, 2, A_HEADS, A_DH))
            outs_s["av"].append(va.reshape(bs, ts, A_HEADS, A_DV))
            outs_s["ckv"].append(ckv3)

            hp = _mix_out(_even_out_kernel, hp, mix_p, [wuv, wout, g1], tm_p, "even_out_proj")
            hs = _mix_out(_even_out_kernel, hs, mix_s, [wuv, wout, g1], tm_s, "even_out_proj")
        else:
            o = li // 2
            w = w_in_odd[o]
            nq, nk = C_HEADS * C_DH, C_KV_HEADS * C_DH
            wq_perm = w[:, :nq].reshape(d, C_HEADS, C_DH)[:, jnp.array(_HEAD_OF_SLOT), :].reshape(d, nq)
            win = jnp.concatenate([wq_perm, w[:, nq:nq + 2 * nk]], axis=1).astype(BF16)
            wft = jnp.transpose(w[:, nq + 2 * nk:]).astype(BF16)
            bf = b_forget[o].reshape(C_HEADS, 1)
            wout = w_out_odd[o].reshape(C_HEADS, C_DH, d)[jnp.array(_HEAD_OF_SLOT)].reshape(nq, d).astype(BF16)

            q, k, kb, v, vb, lft, ct, cn = _odd_in(hp, g0, win, wft, bf, tm_p, LOG2E)
            op = _prompt_attn(_prompt_c_kernel, [q, ct, kb, vb, cn],
                              [_q_tile_spec(q, tq), pl.BlockSpec((1, C_HEADS, tq), lambda b, i: (b, 0, i)),
                               _seq_spec(kb), _seq_spec(vb), _seq_spec(cn)],
                              bp, tp, 1024, 512, [(C_DH, 2 * tq)] * 8, tq, "prompt_attn_c")
            outs_p["ck"].append(k.reshape(bp, tp, C_KV_HEADS, C_DH))
            outs_p["cv"].append(v.reshape(bp, tp, C_KV_HEADS, C_DH))
            outs_p["cf"].append(jnp.swapaxes(lft, 1, 2))

            q, k, kb, v, vb, lft, ct, cn = _odd_in(hs, g0, win, wft, bf, tm_s, 1.0)
            qh = q.reshape(bs, ts, C_HEADS, C_DH)[:, :, jnp.array(_SLOT_OF_HEAD), :]
            qh = jnp.swapaxes(qh, 1, 2)
            selc = (jnp.arange(C_HEADS)[:, None] // 2 == jnp.arange(C_KV_HEADS)[None, :])
            q_bd = jnp.where(selc[None, :, None, :, None], qh[:, :, :, None, :], jnp.zeros((), BF16))
            q_bd = q_bd.reshape(bs, C_HEADS * ts, 512)
            k3, v3 = k.reshape(bs, ts, 512), v.reshape(bs, ts, 512)
            lf3 = jnp.swapaxes(lft.reshape(C_HEADS, bs, ts), 0, 1)
            (os_,) = _decode_attn(functools.partial(_decode_c_kernel, n_pp), page_table, [q_bd],
                                  [ckT[o:o + 1], cvT[o:o + 1], cfT[o:o + 1]],
                                  [_feature_major_page(kb.reshape(bs, ts, 512)),
                                   _feature_major_page(vb.reshape(bs, ts, 512)), _pad_axis(lf3, 2, PAGE)],
                                  [tri_ge, tri_le], [1024],
                                  _softmax_scratch(C_HEADS * 8, 512) * DECODE_SPLIT
                                  + [pltpu.VMEM((C_HEADS, 1), F32), pltpu.VMEM((C_HEADS * 8, 1), F32)],
                                  n_pp, True, "decode_attn_c")
            outs_s["ck"].append(k3.reshape(bs, ts, C_KV_HEADS, C_DH))
            outs_s["cv"].append(v3.reshape(bs, ts, C_KV_HEADS, C_DH))
            outs_s["cf"].append(jnp.swapaxes(lf3, 1, 2))

            hp = _mix_out(_odd_out_kernel, hp, (op,), [wout, g1], tm_p, "odd_out_proj")
            hs = _mix_out(_odd_out_kernel, hs, (os_.reshape(1, ns, nq),), [wout, g1], tm_s, "odd_out_proj")

        wg, wu, wd = ffn_w_gate[li].astype(BF16), ffn_w_up[li].astype(BF16), ffn_w_down[li].astype(BF16)
        cw, cb = ffn_conv_w[li], ffn_conv_b[li].reshape(1, dff)
        hp, tail = _ffn(hp, g2, wg, wu, cw, cb, wd, g3, None, tm_p, tf)
        outs_p["conv"].append(tail[:, 8 - (CONV_W - 1):])
        buf = state_conv[li]
        prev2 = _pad_axis(buf, 1, ts).reshape(1, ns, dff)
        prev1 = _pad_axis(buf[:, 1:], 1, ts).reshape(1, ns, dff)
        hs, gate = _ffn(hs, g2, wg, wu, cw, cb, wd, g3, (prev2, prev1), tm_s, tf)
        outs_s["conv"].append(gate.reshape(bs, ts, dff)[:, ts - (CONV_W - 1):])

    st = lambda xs: jnp.stack(xs)
    res = [hp, hs.reshape(bs, ts, d)]
    for key in ("ak", "av", "ckv", "ck", "cv", "cf", "conv"):
        res += [st(outs_p[key]), st(outs_s[key])]
    return tuple(res)
```

```python
import functools
import math

import jax
import jax.numpy as jnp
from jax import lax
from jax.experimental import pallas as pl
from jax.experimental.pallas import tpu as pltpu

F32 = jnp.float32
BF16 = jnp.bfloat16

EPS = 1e-6
ROPE_THETA = 500000.0
A_HEADS, A_DH, A_DV, A_ROT = 4, 64, 128, 16
B_HEADS, Q_LORA, KV_LORA, QK_NOPE, QK_ROPE, B_DV = 4, 384, 256, 128, 64, 128
C_HEADS, C_KV_HEADS, C_DH = 16, 8, 64
CONV_W = 3
PAGE = 128
LANES = 128
FLASH_COLS = 128
DECODE_SPLIT = 1
LOG2E = math.log2(math.e)
NEG = -1e30
VMEM_LIMIT_BYTES = 56 * 1024 * 1024

_NT = (((1,), (1,)), ((), ()))


def _cp(sem):
    return pltpu.CompilerParams(dimension_semantics=sem, vmem_limit_bytes=VMEM_LIMIT_BYTES)


def _rms(x, g):
    return x * lax.rsqrt(jnp.mean(x * x, axis=-1, keepdims=True) + EPS) * g


def _dot(a, b):
    return jnp.dot(a, b, preferred_element_type=F32)


def _dot_nt(a, b):
    return lax.dot_general(a, b, _NT, preferred_element_type=F32)


def _rope_chunk(x, c, sl, sr, shift):
    return x * c + pltpu.roll(x, LANES - shift, 1) * sl + pltpu.roll(x, shift, 1) * sr


def _rope_tables(pos, rot):
    half = rot // 2
    inv = jnp.power(jnp.float32(ROPE_THETA), -jnp.arange(half, dtype=F32) / half)
    ang = pos.astype(F32)[:, None] * inv[None, :]
    cos, sin = jnp.cos(ang), jnp.sin(ang)
    t = pos.shape[0]
    one = jnp.ones((t, 64 - rot), F32)
    z_rest = jnp.zeros((t, 64 - rot), F32)
    z_half = jnp.zeros((t, half), F32)
    c = jnp.concatenate([cos, cos, one], axis=1)
    sl = jnp.concatenate([-sin, z_half, z_rest], axis=1)
    sr = jnp.concatenate([z_half, sin, z_rest], axis=1)
    return tuple(jnp.tile(a, (1, 2)) for a in (c, sl, sr))


def _even_in_kernel(logit_unit, h_ref, g0_ref, win_ref, gq_ref, gkv_ref, wq_ref, wuk_ref,
                    ca_ref, sla_ref, sra_ref, cb_ref, slb_ref, srb_ref,
                    qa_ref, ka_ref, kab_ref, va_ref, vab_ref, qlat_ref, qrp_ref, ckv_ref, ckvb_ref):
    hn = _rms(h_ref[0], g0_ref[...]).astype(BF16)
    z = _dot(hn, win_ref[...])
    ca, sla, sra = ca_ref[...], sla_ref[...], sra_ref[...]
    cb, slb, srb = cb_ref[...], slb_ref[...], srb_ref[...]
    for j in range(4):
        sl = slice(LANES * j, LANES * (j + 1))
        qa_ref[0, :, sl] = (_rope_chunk(z[:, sl], ca, sla, sra, A_ROT // 2)
                            * (A_DH ** -0.5 * logit_unit)).astype(BF16)
        kr = _rope_chunk(z[:, 512 + LANES * j:512 + LANES * (j + 1)], ca, sla, sra, A_ROT // 2)
        ka_ref[0, :, sl] = kr
        kab_ref[0, :, sl] = kr.astype(BF16)
    va = z[:, 1024:1536]
    tm = va.shape[0]
    for h in range(A_HEADS):
        va_ref[0, pl.ds(h, tm, stride=A_HEADS), :] = va[:, A_DV * h:A_DV * (h + 1)]
    vab_ref[0] = va.astype(BF16)
    cqn = _rms(z[:, 1536:1920], gq_ref[...]).astype(BF16)
    q2 = _dot(cqn, wq_ref[...])
    for h in range(B_HEADS):
        qn = q2[:, LANES * h:LANES * (h + 1)].astype(BF16)
        qlat_ref[0, :, KV_LORA * h:KV_LORA * (h + 1)] = _dot(qn, wuk_ref[h]).astype(BF16)
        qr = _rope_chunk(q2[:, 512 + LANES * h:512 + LANES * (h + 1)], cb, slb, srb, QK_ROPE // 2)
        qrp_ref[0, :, LANES * h:LANES * (h + 1)] = qr.astype(BF16)
    cn = _rms(z[:, 1920:2176], gkv_ref[...])
    krr = _rope_chunk(z[:, 2176:2304], cb, slb, srb, QK_ROPE // 2)
    ckv_ref[0, :, 0:KV_LORA] = cn
    ckv_ref[0, :, KV_LORA:KV_LORA + QK_ROPE] = krr[:, :QK_ROPE]
    ckvb_ref[0, :, 0:KV_LORA] = cn.astype(BF16)
    ckvb_ref[0, :, KV_LORA:KV_LORA + LANES] = krr.astype(BF16)


def _tab_spec(tm, n_tab_tiles):
    return pl.BlockSpec((tm, LANES), lambda b, i: (i % n_tab_tiles, 0))


def _full_spec(shape):
    nd = len(shape)
    return pl.BlockSpec(shape, lambda b, i: (0,) * nd)


def _even_in(h, g0, win, gq, gkv, wq, wuk, tabs_a, tabs_b, tm, logit_unit):
    bk, tk, d = h.shape
    n_tab = tabs_a[0].shape[0] // tm
    tok = lambda w: pl.BlockSpec((1, tm, w), lambda b, i: (b, i, 0))
    out_shapes = [
        jax.ShapeDtypeStruct((bk, tk, 512), BF16),
        jax.ShapeDtypeStruct((bk, tk, 512), F32),
        jax.ShapeDtypeStruct((bk, tk, 512), BF16),
        jax.ShapeDtypeStruct((bk, tk * A_HEADS, A_DV), F32),
        jax.ShapeDtypeStruct((bk, tk, 512), BF16),
        jax.ShapeDtypeStruct((bk, tk, 1024), BF16),
        jax.ShapeDtypeStruct((bk, tk, 512), BF16),
        jax.ShapeDtypeStruct((bk, tk, 320), F32),
        jax.ShapeDtypeStruct((bk, tk, 384), BF16),
    ]
    return pl.pallas_call(
        functools.partial(_even_in_kernel, logit_unit),
        grid=(bk, tk // tm),
        in_specs=[tok(d), _full_spec(g0.shape), _full_spec(win.shape), _full_spec(gq.shape),
                  _full_spec(gkv.shape), _full_spec(wq.shape), _full_spec(wuk.shape)]
                 + [_tab_spec(tm, n_tab)] * 6,
        out_specs=[pl.BlockSpec((1, tm * (s.shape[1] // tk), s.shape[2]), lambda b, i: (b, i, 0))
                   for s in out_shapes],
        out_shape=out_shapes,
        compiler_params=_cp(("parallel", "parallel")),
        name="even_in_proj",
    )(h, g0, win, gq, gkv, wq, wuk, *tabs_a, *tabs_b)


def _split3(x):
    hi = x.astype(BF16)
    r = x - hi.astype(F32)
    mid = r.astype(BF16)
    lo = (r - mid.astype(F32)).astype(BF16)
    return hi, mid, lo


def _tri_dot(x, tri):
    n = x.shape[0]
    hi, mid, lo = _split3(x)
    y = _dot(jnp.concatenate([hi, mid, lo], axis=0), tri)
    return y[0:n] + y[n:2 * n] + y[2 * n:3 * n]


def _log_sigmoid(x):
    return -(jnp.maximum(-x, 0.0) + jnp.log1p(jnp.exp(-jnp.abs(x))))


def _odd_in_kernel(logit_unit, h_ref, g0_ref, win_ref, wft_ref, bf_ref, tri_ref,
                   q_ref, k_ref, kb_ref, v_ref, vb_ref, lft_ref, ct_ref, cn_ref, carry_ref):
    i = pl.program_id(1)
    hn = _rms(h_ref[0], g0_ref[...]).astype(BF16)
    z = _dot(hn, win_ref[...])
    q_ref[0] = (z[:, 0:1024] * (C_DH ** -0.5 * logit_unit)).astype(BF16)
    k = z[:, 1024:1536]
    k_ref[0] = k
    kb_ref[0] = k.astype(BF16)
    v = z[:, 1536:2048]
    v_ref[0] = v
    vb_ref[0] = v.astype(BF16)
    ft = _dot_nt(wft_ref[...], hn)
    lft = _log_sigmoid(ft + bf_ref[...])
    lft_ref[0] = lft

    @pl.when(i == 0)
    def _():
        carry_ref[...] = jnp.zeros_like(carry_ref)

    c = _tri_dot(lft, tri_ref[...]) + carry_ref[...]
    carry_ref[...] = c[:, -1:]
    c = c * logit_unit
    ct_ref[0] = c
    c128 = jnp.concatenate([c, jnp.zeros((LANES - C_HEADS, c.shape[1]), F32)], axis=0)
    cn_ref[0] = jnp.transpose(c128)[:, 0:C_HEADS]


def _odd_in(h, g0, win, wft, bf, tm, logit_unit):
    bk, tk, d = h.shape
    tri = jnp.triu(jnp.ones((tm, tm), F32)).astype(BF16)
    tok = lambda w: pl.BlockSpec((1, tm, w), lambda b, i: (b, i, 0))
    tokt = pl.BlockSpec((1, C_HEADS, tm), lambda b, i: (b, 0, i))
    out_shapes = [
        jax.ShapeDtypeStruct((bk, tk, 1024), BF16),
        jax.ShapeDtypeStruct((bk, tk, 512), F32),
        jax.ShapeDtypeStruct((bk, tk, 512), BF16),
        jax.ShapeDtypeStruct((bk, tk, 512), F32),
        jax.ShapeDtypeStruct((bk, tk, 512), BF16),
        jax.ShapeDtypeStruct((bk, C_HEADS, tk), F32),
        jax.ShapeDtypeStruct((bk, C_HEADS, tk), F32),
        jax.ShapeDtypeStruct((bk, tk, C_HEADS), F32),
    ]
    return pl.pallas_call(
        functools.partial(_odd_in_kernel, logit_unit),
        grid=(bk, tk // tm),
        in_specs=[tok(d), _full_spec(g0.shape), _full_spec(win.shape), _full_spec(wft.shape),
                  _full_spec(bf.shape), _full_spec(tri.shape)],
        out_specs=[tok(1024), tok(512), tok(512), tok(512), tok(512), tokt, tokt, tok(C_HEADS)],
        out_shape=out_shapes,
        scratch_shapes=[pltpu.VMEM((C_HEADS, 1), F32)],
        compiler_params=_cp(("parallel", "arbitrary")),
        name="odd_in_proj",
    )(h, g0, win, wft, bf, tri)


def _mixer_out(acts, consts):
    if len(acts) == 2:
        oa_ref, olat_ref = acts
        wuv_ref, wout_ref, g1_ref = consts
        parts = [oa_ref[0]]
        for hd in range(B_HEADS):
            parts.append(_dot(olat_ref[0, :, KV_LORA * hd:KV_LORA * (hd + 1)], wuv_ref[hd]).astype(BF16))
        y = _dot(jnp.concatenate(parts, axis=1), wout_ref[...])
    else:
        (o_ref,) = acts
        wout_ref, g1_ref = consts
        y = _dot(o_ref[0], wout_ref[...])
    return _rms(y, g1_ref[...])


def _ffn_kernel(short_seq, n_acts, *refs):
    h_ref, acts, consts = refs[0], refs[1:1 + n_acts], refs[1 + n_acts:2 + 2 * n_acts]
    refs = refs[2 + 2 * n_acts:]
    if short_seq:
        (g2_ref, wg_ref, wu_ref, cw_ref, cb_ref, wd_ref, g3_ref, pa_ref, pb_ref,
         out_ref, gate_ref, hn_ref, acc_ref, h1_ref) = refs
    else:
        (g2_ref, wg_ref, wu_ref, cw_ref, cb_ref, wd_ref, g3_ref,
         out_ref, tail_ref, hn_ref, acc_ref, h1_ref, carry_ref) = refs
    i = pl.program_id(1)
    c = pl.program_id(2)

    @pl.when(c == 0)
    def _():
        h1 = h_ref[0] + _mixer_out(acts, consts)
        h1_ref[...] = h1
        hn_ref[...] = _rms(h1, g2_ref[...]).astype(BF16)
        acc_ref[...] = jnp.zeros_like(acc_ref)

    hn = hn_ref[...]
    g = _dot(hn, wg_ref[...])
    u = _dot(hn, wu_ref[...])
    tm = g.shape[0]
    row = lax.broadcasted_iota(jnp.int32, g.shape, 0)
    r1 = pltpu.roll(g, 1, 0)
    r2 = pltpu.roll(g, 2, 0)
    if short_seq:
        t = row % 8
        gm1 = jnp.where(t == 0, pb_ref[0], r1)
        gm2 = jnp.where(t < 2, pa_ref[0], r2)
        gate_ref[0] = g
    else:
        top = jnp.where(i == 0, 0.0, carry_ref[c])
        gm1 = jnp.where(row == 0, top[7:8], r1)
        gm2 = jnp.where(row == 0, top[6:7], jnp.where(row == 1, top[7:8], r2))
        carry_ref[c] = g[tm - 8:tm]
        tail_ref[0] = g[tm - 8:tm]
    cw = cw_ref[...]
    gc = cb_ref[...] + ((cw[0:1] * gm2 + cw[1:2] * gm1) + cw[2:3] * g)
    act = (jax.nn.silu(gc) * u).astype(BF16)
    acc_ref[...] += _dot(act, wd_ref[...])

    @pl.when(c == pl.num_programs(2) - 1)
    def _():
        out_ref[0] = h1_ref[...] + _rms(acc_ref[...], g3_ref[...])


def _ffn(h, acts, mix_consts, g2, wg, wu, cw, cb, wd, g3, prev, tm, tf):
    bk, tk, d = h.shape
    dff = wg.shape[1]
    nc = dff // tf
    short_seq = prev is not None
    assert len(mix_consts) == len(acts) + 1
    tok = pl.BlockSpec((1, tm, d), lambda b, i, c: (b, i, 0))
    const = lambda shape: pl.BlockSpec(shape, lambda b, i, c: (0,) * len(shape))
    in_specs = ([tok] + [pl.BlockSpec((1, tm, a.shape[-1]), lambda b, i, c: (b, i, 0)) for a in acts]
                + [const(w.shape) for w in mix_consts]
                + [const(g2.shape),
                   pl.BlockSpec((d, tf), lambda b, i, c: (0, c)),
                   pl.BlockSpec((d, tf), lambda b, i, c: (0, c)),
                   pl.BlockSpec((CONV_W, tf), lambda b, i, c: (0, c)),
                   pl.BlockSpec((1, tf), lambda b, i, c: (0, c)),
                   pl.BlockSpec((tf, d), lambda b, i, c: (c, 0)),
                   const(g3.shape)])
    args = [h, *acts, *mix_consts, g2, wg, wu, cw, cb, wd, g3]
    scratch = [pltpu.VMEM((tm, d), BF16), pltpu.VMEM((tm, d), F32), pltpu.VMEM((tm, d), F32)]
    if short_seq:
        in_specs += [pl.BlockSpec((1, tm, tf), lambda b, i, c: (b, i, c))] * 2
        args += list(prev)
        out_specs = [tok, pl.BlockSpec((1, tm, tf), lambda b, i, c: (b, i, c))]
        out_shape = [jax.ShapeDtypeStruct(h.shape, F32), jax.ShapeDtypeStruct((bk, tk, dff), F32)]
    else:
        out_specs = [tok, pl.BlockSpec((1, 8, tf), lambda b, i, c: (b, 0, c))]
        out_shape = [jax.ShapeDtypeStruct(h.shape, F32), jax.ShapeDtypeStruct((bk, 8, dff), F32)]
        scratch.append(pltpu.VMEM((nc, 8, tf), F32))
    return pl.pallas_call(
        functools.partial(_ffn_kernel, short_seq, len(acts)),
        grid=(bk, tk // tm, nc),
        in_specs=in_specs,
        out_specs=out_specs,
        out_shape=out_shape,
        scratch_shapes=scratch,
        compiler_params=_cp(("parallel", "arbitrary", "arbitrary")),
        name="conv_ffn",
    )(*args)


def _flash_cols(i, tq, chains):
    subs = []
    for c in chains:
        c["acc"][...] = jnp.zeros_like(c["acc"])
        for c0 in range(0, c["acc"].shape[1], c["cw"]):
            subs.append((c, slice(c0, c0 + c["cw"])))

    def step(j, stats, masked):
        kv = pl.ds(pl.multiple_of(j * tq, tq), tq)
        new_stats = []
        for (c, cols), (m_prev, l_prev) in zip(subs, stats):
            s = c["qk"](kv, cols)
            if c["scale"] is not None:
                s = s * c["scale"]
            if c["bias"] is not None:
                s = s + c["bias"](kv, cols)
            if masked:
                key = lax.broadcasted_iota(jnp.int32, s.shape, 0)
                qpos = (lax.broadcasted_iota(jnp.int32, s.shape, 1) + cols.start) % tq
                s = jnp.where(key <= qpos, s, NEG)
            m_new = jnp.maximum(m_prev, jnp.max(s, axis=0, keepdims=True))
            alpha = jnp.exp2(m_prev - m_new)
            p = jnp.exp2(s - m_new)
            c["acc"][:, cols] = alpha * c["acc"][:, cols] + _dot(c["vt"](kv), p.astype(BF16))
            new_stats.append((m_new, alpha * l_prev + jnp.sum(p, axis=0, keepdims=True)))
        return tuple(new_stats)

    init = tuple((jnp.full((1, c["cw"]), NEG, F32), jnp.zeros((1, c["cw"]), F32)) for c, _ in subs)
    stats = lax.fori_loop(0, i, lambda j, st: step(j, st, False), init)
    stats = step(i, stats, True)
    outs, k = [], 0
    for c in chains:
        n = c["acc"].shape[1] // c["cw"]
        l = jnp.concatenate([stats[k + t][1] for t in range(n)], axis=1)
        outs.append(c["acc"][...] / l)
        k += n
    return outs


def _transpose_bf16(x):
    return jnp.transpose(x.astype(F32)).astype(BF16)


def _fill_transposed(i, tq, src_fn, dst_ref):
    n_tiles = dst_ref.shape[1] // tq

    @pl.when(i == 0)
    def _():
        def body(c, carry):
            rows = pl.ds(pl.multiple_of(c * tq, tq), tq)
            dst_ref[:, rows] = _transpose_bf16(src_fn(rows))
            return carry

        lax.fori_loop(0, n_tiles, body, 0)


def _chain_scratch(dv, r):
    return [pltpu.VMEM((dv, r), F32)]


def _lambda_value(lam_ref, lam_init):
    lam = lam_ref[...]
    s01 = jnp.sum(lam[0:1] * lam[1:2], axis=-1, keepdims=True)
    s23 = jnp.sum(lam[2:3] * lam[3:4], axis=-1, keepdims=True)
    return jnp.exp(s01) - jnp.exp(s23) + lam_init


def _half_rows(blk, half):
    z = jnp.zeros_like(blk)
    return jnp.concatenate([blk, z] if half == 0 else [z, blk], axis=0)


def _prompt_a_kernel(lam_init, q_ref, k_ref, v_ref, lam_ref, gsub_ref, out_ref, vt_ref, *st):
    i = pl.program_id(1)
    tq = q_ref.shape[1]
    _fill_transposed(i, tq, lambda rows: v_ref[0, rows, :], vt_ref)
    qt = _transpose_bf16(q_ref[0])
    lam_val = _lambda_value(lam_ref, lam_init)
    for heads in (tuple(range(A_HEADS)),):
        chains = []
        for hd in heads:
            slots = [s * A_HEADS + hd for s in range(2)]
            qts = [_half_rows(qt[64 * e:64 * (e + 1)], e % 2) for e in slots]
            ksl = [slice(LANES * (e // 2), LANES * (e // 2 + 1)) for e in slots]

            def qk(kv, cols, qts=qts, ksl=ksl):
                s, lo = divmod(cols.start, tq)
                return _dot(k_ref[0, kv, ksl[s]], qts[s][:, lo:lo + cols.stop - cols.start])

            def vt(kv, hd=hd):
                return vt_ref[A_DV * hd:A_DV * (hd + 1), kv]

            chains.append(dict(qk=qk, vt=vt, scale=None, bias=None, acc=st[hd], cw=FLASH_COLS))
        for hd, ot in zip(heads, _flash_cols(i, tq, chains)):
            d = jnp.transpose(ot[:, 0:tq] - lam_val * ot[:, tq:2 * tq])
            out_ref[0, :, A_DV * hd:A_DV * (hd + 1)] = (_rms(d, gsub_ref[...]) * (1.0 - lam_init)).astype(BF16)


def _prompt_b_kernel(q1_ref, q2_ref, c_ref, out_ref, vt_ref, acc_ref):
    i = pl.program_id(1)
    tq = q1_ref.shape[1]
    _fill_transposed(i, tq, lambda rows: c_ref[0, rows, 0:KV_LORA], vt_ref)
    q1t = _transpose_bf16(q1_ref[0])
    q2t = _transpose_bf16(q2_ref[0])
    q1s = jnp.concatenate([q1t[KV_LORA * hd:KV_LORA * (hd + 1)] for hd in range(B_HEADS)], axis=1)
    q2s = jnp.concatenate([q2t[LANES * hd:LANES * (hd + 1)] for hd in range(B_HEADS)], axis=1)

    def qk(kv, cols):
        return (_dot(c_ref[0, kv, 0:KV_LORA], q1s[:, cols])
                + _dot(c_ref[0, kv, KV_LORA:KV_LORA + LANES], q2s[:, cols]))

    chain = dict(qk=qk, vt=lambda kv: vt_ref[:, kv], scale=(QK_NOPE + QK_ROPE) ** -0.5 * LOG2E, bias=None,
                 acc=acc_ref,
                 cw=B_HEADS * tq)
    (ot,) = _flash_cols(i, tq, [chain])
    for hd in range(B_HEADS):
        out_ref[0, :, KV_LORA * hd:KV_LORA * (hd + 1)] = jnp.transpose(ot[:, tq * hd:tq * (hd + 1)]).astype(BF16)


def _prompt_c_kernel(q_ref, ct_ref, k_ref, v_ref, cn_ref, out_ref, vt_ref, *st):
    i = pl.program_id(1)
    tq = q_ref.shape[1]
    _fill_transposed(i, tq, lambda rows: v_ref[0, rows, :], vt_ref)
    qt = _transpose_bf16(q_ref[0])
    n_pairs = len(st) // 2
    for pp0 in range(0, C_KV_HEADS // 2, n_pairs):
        chains = []
        for pp in range(pp0, pp0 + n_pairs):
            for half in range(2):
                heads = [4 * pp + 2 * half + r for r in range(2)]
                blk = jnp.concatenate(
                    [qt[LANES * (2 * pp + r) + 64 * half:LANES * (2 * pp + r) + 64 * (half + 1)]
                     for r in range(2)], axis=1)
                qs = _half_rows(blk, half)
                cq = [ct_ref[0, hd:hd + 1, :] for hd in heads]

                def bias(kv, cols, heads=heads, cq=cq):
                    return jnp.concatenate([cq[r] - cn_ref[0, kv, hd:hd + 1] for r, hd in enumerate(heads)],
                                           axis=1)

                def qk(kv, cols, qs=qs, pp=pp):
                    return _dot(k_ref[0, kv, LANES * pp:LANES * (pp + 1)], qs[:, cols])

                def vt(kv, g=2 * pp + half):
                    return vt_ref[C_DH * g:C_DH * (g + 1), kv]

                chains.append(dict(qk=qk, vt=vt, scale=None, bias=bias, acc=st[2 * (pp - pp0) + half],
                                   cw=2 * tq))
        outs = _flash_cols(i, tq, chains)
        for pp in range(pp0, pp0 + n_pairs):
            ot = jnp.concatenate(outs[2 * (pp - pp0):2 * (pp - pp0) + 2], axis=0)
            for r in range(2):
                out_ref[0, :, LANES * (2 * pp + r):LANES * (2 * pp + r + 1)] = jnp.transpose(
                    ot[:, tq * r:tq * (r + 1)]).astype(BF16)


def _prompt_attn(kern, args, in_specs, bk, t, out_w, vt_rows, chain_shapes, tq, name):
    scratch = [pltpu.VMEM((vt_rows, t), BF16)]
    for dv, r in chain_shapes:
        scratch += _chain_scratch(dv, r)
    return pl.pallas_call(
        kern,
        grid=(bk, t // tq),
        in_specs=in_specs,
        out_specs=pl.BlockSpec((1, tq, out_w), lambda b, i: (b, i, 0)),
        out_shape=jax.ShapeDtypeStruct((bk, t, out_w), BF16),
        scratch_shapes=scratch,
        compiler_params=_cp(("parallel", "arbitrary")),
        name=name,
    )(*args)


def _q_tile_spec(a, tq):
    return pl.BlockSpec((1, tq, a.shape[-1]), lambda b, i: (b, i, 0))


def _seq_spec(a):
    return pl.BlockSpec((1,) + a.shape[1:], lambda b, i: (b, 0, 0))


def _online_update(s_list, pv_fn, m_ref, l_ref, acc_ref):
    m_prev = m_ref[...]
    m_new = m_prev
    for s in s_list:
        m_new = jnp.maximum(m_new, jnp.max(s, axis=-1, keepdims=True))
    alpha = jnp.exp(m_prev - m_new)
    ps = [jnp.exp(s - m_new) for s in s_list]
    l_new = alpha * l_ref[...]
    for p in ps:
        l_new = l_new + jnp.sum(p, axis=-1, keepdims=True)
    l_ref[...] = l_new
    acc_ref[...] = alpha * acc_ref[...] + pv_fn([p.astype(BF16) for p in ps])
    m_ref[...] = m_new


def _page_groups(n_pp):
    per = -(-n_pp // DECODE_SPLIT)
    return [list(range(g * per, min((g + 1) * per, n_pp))) for g in range(DECODE_SPLIT)]


def _merged_softmax(states):
    ms = [m[...] for m, _, _ in states]
    m_all = functools.reduce(jnp.maximum, ms)
    num, den = None, None
    for m, (_, l_ref, acc_ref) in zip(ms, states):
        w = jnp.exp(m - m_all)
        num = acc_ref[...] * w if num is None else num + acc_ref[...] * w
        den = l_ref[...] * w if den is None else den + l_ref[...] * w
    return num / den


def _init_softmax(m_ref, l_ref, acc_ref):
    m_ref[...] = jnp.full_like(m_ref, NEG)
    l_ref[...] = jnp.zeros_like(l_ref)
    acc_ref[...] = jnp.zeros_like(acc_ref)


def _new_token_mask(s):
    row = lax.broadcasted_iota(jnp.int32, s.shape, 0) % 8
    col = lax.broadcasted_iota(jnp.int32, s.shape, 1)
    return jnp.where(col <= row, s, NEG)


def _decode_ab_kernel(n_pp, lam_init, pt_ref, qa_ref, qb_ref, *refs):
    ak_refs, av_refs, bc_refs = refs[:n_pp], refs[n_pp:2 * n_pp], refs[2 * n_pp:3 * n_pp]
    kn_ref, vn_ref, cn_ref, lam_ref, gsub_ref, oa_ref, ob_ref = refs[3 * n_pp:3 * n_pp + 7]
    st = refs[3 * n_pp + 7:]
    sa = [st[3 * g:3 * g + 3] for g in range(DECODE_SPLIT)]
    sb = [st[3 * (DECODE_SPLIT + g):3 * (DECODE_SPLIT + g) + 3] for g in range(DECODE_SPLIT)]
    j = pl.program_id(1)
    qa = qa_ref[0]
    qb = qb_ref[0]
    scale_b = (QK_NOPE + QK_ROPE) ** -0.5

    def update_a(state, k_pages, v_of, mask):
        s_list = [_dot(qa, k) for k in k_pages]
        if mask:
            s_list = [_new_token_mask(s) for s in s_list]

        def pv(ps):
            outs = []
            for hd in range(A_HEADS):
                o = None
                for p, pr in enumerate(ps):
                    t = _dot(pr[16 * hd:16 * (hd + 1)], v_of(p, hd))
                    o = t if o is None else o + t
                outs.append(o)
            return jnp.concatenate(outs, axis=0)

        _online_update(s_list, pv, *state)

    def update_b(state, pages, mask):
        s_list = [_dot(qb, c) * scale_b for c in pages]
        if mask:
            s_list = [_new_token_mask(s) for s in s_list]

        def pv(ps):
            o = None
            for pr, c in zip(ps, pages):
                t = _dot_nt(pr, c[0:KV_LORA])
                o = t if o is None else o + t
            return o

        _online_update(s_list, pv, *state)

    @pl.when(j == 0)
    def _():
        for state in sa + sb:
            _init_softmax(*state)

    for g, grp in enumerate(_page_groups(n_pp)):
        if not grp:
            continue
        update_a(sa[g], [ak_refs[p][0, 0].astype(BF16) for p in grp],
                 lambda p, hd, grp=grp: av_refs[grp[p]][0, 0, pl.ds(hd, PAGE, stride=A_HEADS), :].astype(BF16),
                 False)
        update_b(sb[g], [bc_refs[p][0, 0].astype(BF16) for p in grp], False)

    @pl.when(j == pl.num_programs(1) - 1)
    def _():
        update_a(sa[0], [kn_ref[0].astype(BF16)],
                 lambda p, hd: vn_ref[0, pl.ds(hd, PAGE, stride=A_HEADS), :].astype(BF16), True)
        update_b(sb[0], [cn_ref[0].astype(BF16)], True)
        o = _merged_softmax(sa)
        lam_val = _lambda_value(lam_ref, lam_init)
        for hd in range(A_HEADS):
            d = o[16 * hd:16 * hd + 8] - lam_val * o[16 * hd + 8:16 * hd + 16]
            oa_ref[0, :, A_DV * hd:A_DV * (hd + 1)] = (
                _rms(d, gsub_ref[...]) * (1.0 - lam_init)).astype(BF16)
        o = _merged_softmax(sb)
        for hd in range(B_HEADS):
            ob_ref[0, :, KV_LORA * hd:KV_LORA * (hd + 1)] = o[8 * hd:8 * (hd + 1)].astype(BF16)


def _expand_heads(x):
    return jnp.broadcast_to(x[:, None, :], (C_HEADS, 8, x.shape[-1])).reshape(C_HEADS * 8, x.shape[-1])


def _decode_c_kernel(n_pp, pt_ref, q_ref, *refs):
    k_refs, v_refs, lf_refs = refs[:n_pp], refs[n_pp:2 * n_pp], refs[2 * n_pp:3 * n_pp]
    kn_ref, vn_ref, lfn_ref, tri_ge_ref, tri_le_ref, out_ref = refs[3 * n_pp:3 * n_pp + 6]
    st = refs[3 * n_pp + 6:]
    states = [st[3 * g:3 * g + 3] for g in range(DECODE_SPLIT)]
    base_ref, bq_ref = st[3 * DECODE_SPLIT:]
    j = pl.program_id(1)
    last = pl.num_programs(1) - 1
    q = q_ref[0]

    def scores(k_page):
        return _dot(q, k_page)

    def pv_of(pages):
        def pv(ps):
            o = None
            for pr, v in zip(ps, pages):
                t = _dot_nt(pr, v)
                o = t if o is None else o + t
            return o
        return pv

    @pl.when(j == 0)
    def _():
        for state in states:
            _init_softmax(*state)
        base_ref[...] = jnp.zeros_like(base_ref)
        cnl = _tri_dot(lfn_ref[0], tri_le_ref[...])
        rep = _expand_heads(cnl)
        row = lax.broadcasted_iota(jnp.int32, rep.shape, 0) % 8
        col = lax.broadcasted_iota(jnp.int32, rep.shape, 1)
        bq = jnp.sum(jnp.where(col == row, rep, 0.0), axis=-1, keepdims=True)
        bq_ref[...] = bq
        s_new = _new_token_mask(scores(kn_ref[0].astype(BF16)) + (bq - rep))
        _online_update([s_new], pv_of([vn_ref[0].astype(BF16)]), *states[0])

    lf = jnp.concatenate([lf_refs[p][0, 0] for p in range(n_pp)], axis=0)
    incl = _tri_dot(lf, tri_ge_ref[...])
    excl = incl - lf
    bq = bq_ref[...]
    base = base_ref[...]
    s_list = []
    for p in range(n_pp):
        sl = slice(C_HEADS * p, C_HEADS * (p + 1))
        bias = _expand_heads(base + excl[sl])
        s_list.append(scores(k_refs[p][0, 0].astype(BF16)) + (bq + bias))
        base = base + incl[sl, 0:1]
    base_ref[...] = base
    for g, grp in enumerate(_page_groups(n_pp)):
        if grp:
            _online_update([s_list[p] for p in grp], pv_of([v_refs[p][0, 0].astype(BF16) for p in grp]),
                           *states[g])

    @pl.when(j == last)
    def _():
        o = _merged_softmax(states)
        lane = lax.broadcasted_iota(jnp.int32, (8, LANES), 1)
        for pp in range(C_KV_HEADS // 2):
            csl = slice(LANES * pp, LANES * (pp + 1))
            for r in range(2):
                h0, h1 = 4 * pp + r, 4 * pp + 2 + r
                chunk = jnp.where(lane < 64, o[8 * h0:8 * (h0 + 1), csl], o[8 * h1:8 * (h1 + 1), csl])
                out_ref[0, :, LANES * (2 * pp + r):LANES * (2 * pp + r + 1)] = chunk.astype(BF16)


def _softmax_scratch(rows, dv):
    return [pltpu.VMEM((rows, 1), F32), pltpu.VMEM((rows, 1), F32), pltpu.VMEM((rows, dv), F32)]


def _decode_attn(kern, page_table, qs, caches, news, consts, out_ws, scratch, n_pp, reverse, name):
    nb, n_pages = page_table.shape
    n_steps = n_pages // n_pp

    def page_spec(c, p):
        def imap(b, j, pt):
            idx = j * n_pp + p
            if reverse:
                idx = n_pages - 1 - idx
            return (0, pt[b, idx], 0, 0)
        return pl.BlockSpec((1, 1) + c.shape[2:], imap)

    in_specs, args = [], []
    for q in qs:
        in_specs.append(pl.BlockSpec((1,) + q.shape[1:], lambda b, j, pt: (b, 0, 0)))
        args.append(q)
    for c in caches:
        for p in range(n_pp):
            in_specs.append(page_spec(c, p))
            args.append(c)
    for a in news:
        in_specs.append(pl.BlockSpec((1,) + a.shape[1:], lambda b, j, pt: (b, 0, 0)))
        args.append(a)
    for a in consts:
        in_specs.append(pl.BlockSpec(a.shape, lambda b, j, pt, nd=a.ndim: (0,) * nd))
        args.append(a)
    grid_spec = pltpu.PrefetchScalarGridSpec(
        num_scalar_prefetch=1,
        grid=(nb, n_steps),
        in_specs=in_specs,
        out_specs=[pl.BlockSpec((1, 8, w), lambda b, j, pt: (b, 0, 0)) for w in out_ws],
        scratch_shapes=scratch,
    )
    return pl.pallas_call(
        kern,
        grid_spec=grid_spec,
        out_shape=[jax.ShapeDtypeStruct((nb, 8, w), BF16) for w in out_ws],
        compiler_params=_cp(("parallel", "arbitrary")),
        name=name,
    )(page_table, *args)


_HEAD_OF_SLOT = [4 * (c // 2) + 2 * half + (c % 2) for c in range(C_HEADS // 2) for half in range(2)]
_SLOT_OF_HEAD = [_HEAD_OF_SLOT.index(h) for h in range(C_HEADS)]


def _lambda_init(layer):
    return 0.8 - 0.6 * math.exp(-0.3 * layer)


def _pad_axis(x, axis, size):
    pad = [(0, 0)] * x.ndim
    pad[axis] = (0, size - x.shape[axis])
    return jnp.pad(x, pad)


def _tok_tile(n, pref):
    tm = min(n, pref)
    assert n % tm == 0 and tm % 8 == 0
    return tm


def _feature_major_page(x):
    return _pad_axis(jnp.swapaxes(x, 1, 2), 2, PAGE)


def kernel(x_prompt, x_sample, cache_a_k, cache_a_v, cache_b_ckv, cache_c_k, cache_c_v, cache_c_logf, state_conv, page_table, norm_gains, w_in_even, w_q_up, w_kv_uk, w_kv_uv, g_q_lat, g_kv_lat, diff_lambda, g_diff_subln, w_out_even, w_in_odd, b_forget, w_out_odd, ffn_w_gate, ffn_w_up, ffn_conv_w, ffn_conv_b, ffn_w_down):
    depth = norm_gains.shape[0]
    bp, tp, d = x_prompt.shape
    bs, ts, _ = x_sample.shape
    n_pages = page_table.shape[1]
    n_pool = cache_a_k.shape[1]
    dff = ffn_w_gate.shape[2]
    assert ts == 8 and cache_a_k.shape[2] == PAGE
    ns = bs * ts
    tm_p, tm_s = _tok_tile(tp, 512), _tok_tile(ns, 512)
    tq = _tok_tile(tp, 256)
    tf = dff // 2 if dff % (2 * LANES) == 0 else dff
    n_pp = 32 if n_pages % 32 == 0 else n_pages
    assert dff % tf == 0

    pos_p = jnp.arange(tp, dtype=jnp.int32)
    pos_s = n_pages * PAGE + jnp.arange(ts, dtype=jnp.int32)
    tabs_p = (_rope_tables(pos_p, A_ROT), _rope_tables(pos_p, QK_ROPE))
    tabs_s = tuple(tuple(jnp.tile(t, (tm_s // ts, 1)) for t in tabs)
                   for tabs in (_rope_tables(pos_s, A_ROT), _rope_tables(pos_s, QK_ROPE)))

    akT = jnp.transpose(cache_a_k, (0, 1, 3, 4, 5, 2)).reshape(-1, n_pool, 512, PAGE)
    av4 = cache_a_v.reshape(-1, n_pool, PAGE * A_HEADS, A_DV)
    bcT = jnp.swapaxes(cache_b_ckv, 2, 3)
    ckT = jnp.transpose(cache_c_k, (0, 1, 3, 4, 2)).reshape(-1, n_pool, 512, PAGE)
    cvT = jnp.transpose(cache_c_v, (0, 1, 3, 4, 2)).reshape(-1, n_pool, 512, PAGE)
    cfT = jnp.swapaxes(cache_c_logf, 2, 3)

    tri_ge = jnp.tril(jnp.ones((PAGE, PAGE), F32)).astype(BF16)
    tri_le = jnp.triu(jnp.ones((PAGE, PAGE), F32)).astype(BF16)

    hp = x_prompt
    hs = x_sample.reshape(1, ns, d)
    outs_p = {k: [] for k in ("ak", "av", "ckv", "ck", "cv", "cf", "conv")}
    outs_s = {k: [] for k in outs_p}

    for li in range(depth):
        gn = norm_gains[li]
        g0, g1, g2, g3 = (gn[k].reshape(1, d) for k in range(4))
        if li % 2 == 0:
            e = li // 2
            lam_init = _lambda_init(li)
            win = _pad_axis(w_in_even[e], 1, 2304).astype(BF16)
            wq_n = w_q_up[e][:, :, :QK_NOPE].reshape(Q_LORA, B_HEADS * QK_NOPE)
            wq_r = _pad_axis(w_q_up[e][:, :, QK_NOPE:], 2, LANES).reshape(Q_LORA, B_HEADS * LANES)
            wq = jnp.concatenate([wq_n, wq_r], axis=1).astype(BF16)
            wuk = jnp.transpose(w_kv_uk[e], (1, 2, 0)).astype(BF16)
            wuv = jnp.transpose(w_kv_uv[e], (1, 0, 2)).astype(BF16)
            wout = w_out_even[e].astype(BF16)
            gq, gkv = g_q_lat[e].reshape(1, -1), g_kv_lat[e].reshape(1, -1)
            lam, gsub = diff_lambda[e], g_diff_subln[e].reshape(1, -1)

            def even_in(h, tabs, tm, logit_unit):
                return _even_in(h, g0, win, gq, gkv, wq, wuk, tabs[0], tabs[1], tm, logit_unit)

            qa, ka, kab, va, vab, qlat, qrp, ckv, ckvb = even_in(hp, tabs_p, tm_p, LOG2E)
            oa = _prompt_attn(functools.partial(_prompt_a_kernel, lam_init), [qa, kab, vab, lam, gsub],
                              [_q_tile_spec(qa, tq), _seq_spec(kab), _seq_spec(vab), _full_spec(lam.shape),
                               _full_spec(gsub.shape)],
                              bp, tp, 512, 512, [(A_DV, 2 * tq)] * A_HEADS, tq, "prompt_attn_a")
            olat = _prompt_attn(_prompt_b_kernel, [qlat, qrp, ckvb],
                                [_q_tile_spec(qlat, tq), _q_tile_spec(qrp, tq), _seq_spec(ckvb)],
                                bp, tp, 1024, KV_LORA, [(KV_LORA, B_HEADS * tq)], tq, "prompt_attn_b")
            mix_p = (oa, olat)
            outs_p["ak"].append(ka.reshape(bp, tp, 2, A_HEADS, A_DH))
            outs_p["av"].append(va.reshape(bp, tp, A_HEADS, A_DV))
            outs_p["ckv"].append(ckv)

            qa, ka, kab, va, vab, qlat, qrp, ckv, ckvb = even_in(hs, tabs_s, tm_s, 1.0)
            sel = (jnp.arange(8)[:, None] == jnp.arange(8).reshape(2, 4).T.reshape(8)[None, :])
            qx = jnp.transpose(qa.reshape(bs, ts, 2, A_HEADS, A_DH), (0, 3, 2, 1, 4)).reshape(bs, 8, ts, A_DH)
            q_bd = jnp.where(sel.T[None, :, None, :, None], qx[:, :, :, None, :], jnp.zeros((), BF16))
            q_bd = q_bd.reshape(bs, 8 * ts, 512)
            ckv3 = ckv.reshape(bs, ts, 320)
            kn = _feature_major_page(kab.reshape(bs, ts, 512))
            vn = _pad_axis(va.reshape(bs, ts * A_HEADS, A_DV), 1, PAGE * A_HEADS)
            cn_page = _feature_major_page(ckvb.reshape(bs, ts, KV_LORA + LANES)[:, :, :KV_LORA + QK_ROPE])
            q_abs = jnp.concatenate([qlat.reshape(bs, ts, B_HEADS, KV_LORA),
                                     qrp.reshape(bs, ts, B_HEADS, LANES)[..., :QK_ROPE]], axis=-1)
            q_abs = jnp.swapaxes(q_abs, 1, 2).reshape(bs, B_HEADS * ts, KV_LORA + QK_ROPE)
            oa, olat = _decode_attn(functools.partial(_decode_ab_kernel, n_pp, lam_init), page_table, [q_bd, q_abs],
                                    [akT[e:e + 1], av4[e:e + 1], bcT[e:e + 1]],
                                    [kn, vn, cn_page], [lam, gsub], [512, 1024],
                                    _softmax_scratch(64, A_DV) * DECODE_SPLIT
                                    + _softmax_scratch(32, KV_LORA) * DECODE_SPLIT, n_pp, False,
                                    "decode_attn_ab")
            mix_s = (oa.reshape(1, ns, 512), olat.reshape(1, ns, 1024))
            outs_s["ak"].append(ka.reshape(bs, ts, 2, A_HEADS, A_DH))
            outs_s["av"].append(va.reshape(bs, ts, A_HEADS, A_DV))
            outs_s["ckv"].append(ckv3)

            mix_consts = [wuv, wout, g1]
        else:
            o = li // 2
            w = w_in_odd[o]
            nq, nk = C_HEADS * C_DH, C_KV_HEADS * C_DH
            wq_perm = w[:, :nq].reshape(d, C_HEADS, C_DH)[:, jnp.array(_HEAD_OF_SLOT), :].reshape(d, nq)
            win = jnp.concatenate([wq_perm, w[:, nq:nq + 2 * nk]], axis=1).astype(BF16)
            wft = jnp.transpose(w[:, nq + 2 * nk:]).astype(BF16)
            bf = b_forget[o].reshape(C_HEADS, 1)
            wout = w_out_odd[o].reshape(C_HEADS, C_DH, d)[jnp.array(_HEAD_OF_SLOT)].reshape(nq, d).astype(BF16)

            q, k, kb, v, vb, lft, ct, cn = _odd_in(hp, g0, win, wft, bf, tm_p, LOG2E)
            op = _prompt_attn(_prompt_c_kernel, [q, ct, kb, vb, cn],
                              [_q_tile_spec(q, tq), pl.BlockSpec((1, C_HEADS, tq), lambda b, i: (b, 0, i)),
                               _seq_spec(kb), _seq_spec(vb), _seq_spec(cn)],
                              bp, tp, 1024, 512, [(C_DH, 2 * tq)] * 8, tq, "prompt_attn_c")
            outs_p["ck"].append(k.reshape(bp, tp, C_KV_HEADS, C_DH))
            outs_p["cv"].append(v.reshape(bp, tp, C_KV_HEADS, C_DH))
            outs_p["cf"].append(jnp.swapaxes(lft, 1, 2))

            q, k, kb, v, vb, lft, ct, cn = _odd_in(hs, g0, win, wft, bf, tm_s, 1.0)
            qh = q.reshape(bs, ts, C_HEADS, C_DH)[:, :, jnp.array(_SLOT_OF_HEAD), :]
            qh = jnp.swapaxes(qh, 1, 2)
            selc = (jnp.arange(C_HEADS)[:, None] // 2 == jnp.arange(C_KV_HEADS)[None, :])
            q_bd = jnp.where(selc[None, :, None, :, None], qh[:, :, :, None, :], jnp.zeros((), BF16))
            q_bd = q_bd.reshape(bs, C_HEADS * ts, 512)
            k3, v3 = k.reshape(bs, ts, 512), v.reshape(bs, ts, 512)
            lf3 = jnp.swapaxes(lft.reshape(C_HEADS, bs, ts), 0, 1)
            (os_,) = _decode_attn(functools.partial(_decode_c_kernel, n_pp), page_table, [q_bd],
                                  [ckT[o:o + 1], cvT[o:o + 1], cfT[o:o + 1]],
                                  [_feature_major_page(kb.reshape(bs, ts, 512)),
                                   _feature_major_page(vb.reshape(bs, ts, 512)), _pad_axis(lf3, 2, PAGE)],
                                  [tri_ge, tri_le], [1024],
                                  _softmax_scratch(C_HEADS * 8, 512) * DECODE_SPLIT
                                  + [pltpu.VMEM((C_HEADS, 1), F32), pltpu.VMEM((C_HEADS * 8, 1), F32)],
                                  n_pp, True, "decode_attn_c")
            outs_s["ck"].append(k3.reshape(bs, ts, C_KV_HEADS, C_DH))
            outs_s["cv"].append(v3.reshape(bs, ts, C_KV_HEADS, C_DH))
            outs_s["cf"].append(jnp.swapaxes(lf3, 1, 2))

            mix_p, mix_s, mix_consts = (op,), (os_.reshape(1, ns, nq),), [wout, g1]

        wg, wu, wd = ffn_w_gate[li].astype(BF16), ffn_w_up[li].astype(BF16), ffn_w_down[li].astype(BF16)
        cw, cb = ffn_conv_w[li], ffn_conv_b[li].reshape(1, dff)
        hp, tail = _ffn(hp, mix_p, mix_consts, g2, wg, wu, cw, cb, wd, g3, None, tm_p, tf)
        outs_p["conv"].append(tail[:, 8 - (CONV_W - 1):])
        buf = state_conv[li]
        prev2 = _pad_axis(buf, 1, ts).reshape(1, ns, dff)
        prev1 = _pad_axis(buf[:, 1:], 1, ts).reshape(1, ns, dff)
        hs, gate = _ffn(hs, mix_s, mix_consts, g2, wg, wu, cw, cb, wd, g3, (prev2, prev1), _tok_tile(ns, 256), tf)
        outs_s["conv"].append(gate.reshape(bs, ts, dff)[:, ts - (CONV_W - 1):])

    st = lambda xs: jnp.stack(xs)
    res = [hp, hs.reshape(bs, ts, d)]
    for key in ("ak", "av", "ckv", "ck", "cv", "cf", "conv"):
        res += [st(outs_p[key]), st(outs_s[key])]
    return tuple(res)
```

```python
import functools
import math

import jax
import jax.numpy as jnp
from jax import lax
from jax.experimental import pallas as pl
from jax.experimental.pallas import tpu as pltpu

F32 = jnp.float32
BF16 = jnp.bfloat16

EPS = 1e-6
ROPE_THETA = 500000.0
A_HEADS, A_DH, A_DV, A_ROT = 4, 64, 128, 16
B_HEADS, Q_LORA, KV_LORA, QK_NOPE, QK_ROPE, B_DV = 4, 384, 256, 128, 64, 128
C_HEADS, C_KV_HEADS, C_DH = 16, 8, 64
CONV_W = 3
PAGE = 128
LANES = 128
FLASH_COLS = 128
DECODE_SPLIT = 1
LOG2E = math.log2(math.e)
NEG = -1e30
VMEM_LIMIT_BYTES = 56 * 1024 * 1024

_NT = (((1,), (1,)), ((), ()))


def _cp(sem):
    return pltpu.CompilerParams(dimension_semantics=sem, vmem_limit_bytes=VMEM_LIMIT_BYTES)


def _rms(x, g):
    return x * lax.rsqrt(jnp.mean(x * x, axis=-1, keepdims=True) + EPS) * g


def _dot(a, b):
    return jnp.dot(a, b, preferred_element_type=F32)


def _dot_nt(a, b):
    return lax.dot_general(a, b, _NT, preferred_element_type=F32)


def _rope_chunk(x, c, sl, sr, shift):
    return x * c + pltpu.roll(x, LANES - shift, 1) * sl + pltpu.roll(x, shift, 1) * sr


def _rope_tables(pos, rot):
    half = rot // 2
    inv = jnp.power(jnp.float32(ROPE_THETA), -jnp.arange(half, dtype=F32) / half)
    ang = pos.astype(F32)[:, None] * inv[None, :]
    cos, sin = jnp.cos(ang), jnp.sin(ang)
    t = pos.shape[0]
    one = jnp.ones((t, 64 - rot), F32)
    z_rest = jnp.zeros((t, 64 - rot), F32)
    z_half = jnp.zeros((t, half), F32)
    c = jnp.concatenate([cos, cos, one], axis=1)
    sl = jnp.concatenate([-sin, z_half, z_rest], axis=1)
    sr = jnp.concatenate([z_half, sin, z_rest], axis=1)
    return tuple(jnp.tile(a, (1, 2)) for a in (c, sl, sr))


def _even_in_kernel(logit_unit, h_ref, g0_ref, win_ref, gq_ref, gkv_ref, wq_ref, wuk_ref,
                    ca_ref, sla_ref, sra_ref, cb_ref, slb_ref, srb_ref,
                    qa_ref, ka_ref, kab_ref, va_ref, vab_ref, qlat_ref, qrp_ref, ckv_ref, ckvb_ref):
    hn = _rms(h_ref[0], g0_ref[...]).astype(BF16)
    z = _dot(hn, win_ref[...])
    ca, sla, sra = ca_ref[...], sla_ref[...], sra_ref[...]
    cb, slb, srb = cb_ref[...], slb_ref[...], srb_ref[...]
    for j in range(4):
        sl = slice(LANES * j, LANES * (j + 1))
        qa_ref[0, :, sl] = (_rope_chunk(z[:, sl], ca, sla, sra, A_ROT // 2)
                            * (A_DH ** -0.5 * logit_unit)).astype(BF16)
        kr = _rope_chunk(z[:, 512 + LANES * j:512 + LANES * (j + 1)], ca, sla, sra, A_ROT // 2)
        ka_ref[0, :, sl] = kr
        kab_ref[0, :, sl] = kr.astype(BF16)
    va = z[:, 1024:1536]
    tm = va.shape[0]
    for h in range(A_HEADS):
        va_ref[0, pl.ds(h, tm, stride=A_HEADS), :] = va[:, A_DV * h:A_DV * (h + 1)]
    vab_ref[0] = va.astype(BF16)
    cqn = _rms(z[:, 1536:1920], gq_ref[...]).astype(BF16)
    q2 = _dot(cqn, wq_ref[...])
    for h in range(B_HEADS):
        qn = q2[:, LANES * h:LANES * (h + 1)].astype(BF16)
        qlat_ref[0, :, KV_LORA * h:KV_LORA * (h + 1)] = _dot(qn, wuk_ref[h]).astype(BF16)
        qr = _rope_chunk(q2[:, 512 + LANES * h:512 + LANES * (h + 1)], cb, slb, srb, QK_ROPE // 2)
        qrp_ref[0, :, LANES * h:LANES * (h + 1)] = qr.astype(BF16)
    cn = _rms(z[:, 1920:2176], gkv_ref[...])
    krr = _rope_chunk(z[:, 2176:2304], cb, slb, srb, QK_ROPE // 2)
    ckv_ref[0, :, 0:KV_LORA] = cn
    ckv_ref[0, :, KV_LORA:KV_LORA + QK_ROPE] = krr[:, :QK_ROPE]
    ckvb_ref[0, :, 0:KV_LORA] = cn.astype(BF16)
    ckvb_ref[0, :, KV_LORA:KV_LORA + LANES] = krr.astype(BF16)


def _tab_spec(tm, n_tab_tiles):
    return pl.BlockSpec((tm, LANES), lambda b, i: (i % n_tab_tiles, 0))


def _full_spec(shape):
    nd = len(shape)
    return pl.BlockSpec(shape, lambda b, i: (0,) * nd)


def _even_in(h, g0, win, gq, gkv, wq, wuk, tabs_a, tabs_b, tm, logit_unit):
    bk, tk, d = h.shape
    n_tab = tabs_a[0].shape[0] // tm
    tok = lambda w: pl.BlockSpec((1, tm, w), lambda b, i: (b, i, 0))
    out_shapes = [
        jax.ShapeDtypeStruct((bk, tk, 512), BF16),
        jax.ShapeDtypeStruct((bk, tk, 512), F32),
        jax.ShapeDtypeStruct((bk, tk, 512), BF16),
        jax.ShapeDtypeStruct((bk, tk * A_HEADS, A_DV), F32),
        jax.ShapeDtypeStruct((bk, tk, 512), BF16),
        jax.ShapeDtypeStruct((bk, tk, 1024), BF16),
        jax.ShapeDtypeStruct((bk, tk, 512), BF16),
        jax.ShapeDtypeStruct((bk, tk, 320), F32),
        jax.ShapeDtypeStruct((bk, tk, 384), BF16),
    ]
    return pl.pallas_call(
        functools.partial(_even_in_kernel, logit_unit),
        grid=(bk, tk // tm),
        in_specs=[tok(d), _full_spec(g0.shape), _full_spec(win.shape), _full_spec(gq.shape),
                  _full_spec(gkv.shape), _full_spec(wq.shape), _full_spec(wuk.shape)]
                 + [_tab_spec(tm, n_tab)] * 6,
        out_specs=[pl.BlockSpec((1, tm * (s.shape[1] // tk), s.shape[2]), lambda b, i: (b, i, 0))
                   for s in out_shapes],
        out_shape=out_shapes,
        compiler_params=_cp(("parallel", "parallel")),
        name="even_in_proj",
    )(h, g0, win, gq, gkv, wq, wuk, *tabs_a, *tabs_b)


def _split3(x):
    hi = x.astype(BF16)
    r = x - hi.astype(F32)
    mid = r.astype(BF16)
    lo = (r - mid.astype(F32)).astype(BF16)
    return hi, mid, lo


def _tri_dot(x, tri):
    n = x.shape[0]
    hi, mid, lo = _split3(x)
    y = _dot(jnp.concatenate([hi, mid, lo], axis=0), tri)
    return y[0:n] + y[n:2 * n] + y[2 * n:3 * n]


def _log_sigmoid(x):
    return -(jnp.maximum(-x, 0.0) + jnp.log1p(jnp.exp(-jnp.abs(x))))


def _odd_in_kernel(logit_unit, h_ref, g0_ref, win_ref, wft_ref, bf_ref, tri_ref,
                   q_ref, k_ref, kb_ref, v_ref, vb_ref, lft_ref, ct_ref, cn_ref, carry_ref):
    i = pl.program_id(1)
    hn = _rms(h_ref[0], g0_ref[...]).astype(BF16)
    z = _dot(hn, win_ref[...])
    q_ref[0] = (z[:, 0:1024] * (C_DH ** -0.5 * logit_unit)).astype(BF16)
    k = z[:, 1024:1536]
    k_ref[0] = k
    kb_ref[0] = k.astype(BF16)
    v = z[:, 1536:2048]
    v_ref[0] = v
    vb_ref[0] = v.astype(BF16)
    ft = _dot_nt(wft_ref[...], hn)
    lft = _log_sigmoid(ft + bf_ref[...])
    lft_ref[0] = lft

    @pl.when(i == 0)
    def _():
        carry_ref[...] = jnp.zeros_like(carry_ref)

    c = _tri_dot(lft, tri_ref[...]) + carry_ref[...]
    carry_ref[...] = c[:, -1:]
    c = c * logit_unit
    ct_ref[0] = c
    c128 = jnp.concatenate([c, jnp.zeros((LANES - C_HEADS, c.shape[1]), F32)], axis=0)
    cn_ref[0] = jnp.transpose(c128)[:, 0:C_HEADS]


def _odd_in(h, g0, win, wft, bf, tm, logit_unit):
    bk, tk, d = h.shape
    tri = jnp.triu(jnp.ones((tm, tm), F32)).astype(BF16)
    tok = lambda w: pl.BlockSpec((1, tm, w), lambda b, i: (b, i, 0))
    tokt = pl.BlockSpec((1, C_HEADS, tm), lambda b, i: (b, 0, i))
    out_shapes = [
        jax.ShapeDtypeStruct((bk, tk, 1024), BF16),
        jax.ShapeDtypeStruct((bk, tk, 512), F32),
        jax.ShapeDtypeStruct((bk, tk, 512), BF16),
        jax.ShapeDtypeStruct((bk, tk, 512), F32),
        jax.ShapeDtypeStruct((bk, tk, 512), BF16),
        jax.ShapeDtypeStruct((bk, C_HEADS, tk), F32),
        jax.ShapeDtypeStruct((bk, C_HEADS, tk), F32),
        jax.ShapeDtypeStruct((bk, tk, C_HEADS), F32),
    ]
    return pl.pallas_call(
        functools.partial(_odd_in_kernel, logit_unit),
        grid=(bk, tk // tm),
        in_specs=[tok(d), _full_spec(g0.shape), _full_spec(win.shape), _full_spec(wft.shape),
                  _full_spec(bf.shape), _full_spec(tri.shape)],
        out_specs=[tok(1024), tok(512), tok(512), tok(512), tok(512), tokt, tokt, tok(C_HEADS)],
        out_shape=out_shapes,
        scratch_shapes=[pltpu.VMEM((C_HEADS, 1), F32)],
        compiler_params=_cp(("parallel", "arbitrary")),
        name="odd_in_proj",
    )(h, g0, win, wft, bf, tri)


def _even_out_kernel(h_ref, oa_ref, olat_ref, wuv_ref, wout_ref, g1_ref, out_ref):
    parts = [oa_ref[0]]
    for hd in range(B_HEADS):
        parts.append(_dot(olat_ref[0, :, KV_LORA * hd:KV_LORA * (hd + 1)], wuv_ref[hd]).astype(BF16))
    y = _dot(jnp.concatenate(parts, axis=1), wout_ref[...])
    out_ref[0] = h_ref[0] + _rms(y, g1_ref[...])


def _odd_out_kernel(h_ref, o_ref, wout_ref, g1_ref, out_ref):
    y = _dot(o_ref[0], wout_ref[...])
    out_ref[0] = h_ref[0] + _rms(y, g1_ref[...])


def _mix_out(kern, h, acts, consts, tm, name):
    bk, tk, d = h.shape
    tok = lambda w: pl.BlockSpec((1, tm, w), lambda b, i: (b, i, 0))
    return pl.pallas_call(
        kern,
        grid=(bk, tk // tm),
        in_specs=[tok(d)] + [tok(a.shape[-1]) for a in acts] + [_full_spec(c.shape) for c in consts],
        out_specs=tok(d),
        out_shape=jax.ShapeDtypeStruct(h.shape, F32),
        compiler_params=_cp(("parallel", "parallel")),
        name=name,
    )(h, *acts, *consts)


def _ffn_kernel(short_seq, *refs):
    if short_seq:
        (h_ref, g2_ref, wg_ref, wu_ref, cw_ref, cb_ref, wd_ref, g3_ref, pa_ref, pb_ref,
         out_ref, gate_ref, hn_ref, acc_ref) = refs
    else:
        (h_ref, g2_ref, wg_ref, wu_ref, cw_ref, cb_ref, wd_ref, g3_ref,
         out_ref, tail_ref, hn_ref, acc_ref, carry_ref) = refs
    i = pl.program_id(1)
    c = pl.program_id(2)

    @pl.when(c == 0)
    def _():
        hn_ref[...] = _rms(h_ref[0], g2_ref[...]).astype(BF16)
        acc_ref[...] = jnp.zeros_like(acc_ref)

    hn = hn_ref[...]
    g = _dot(hn, wg_ref[...])
    u = _dot(hn, wu_ref[...])
    tm = g.shape[0]
    row = lax.broadcasted_iota(jnp.int32, g.shape, 0)
    r1 = pltpu.roll(g, 1, 0)
    r2 = pltpu.roll(g, 2, 0)
    if short_seq:
        t = row % 8
        gm1 = jnp.where(t == 0, pb_ref[0], r1)
        gm2 = jnp.where(t < 2, pa_ref[0], r2)
        gate_ref[0] = g
    else:
        top = jnp.where(i == 0, 0.0, carry_ref[c])
        gm1 = jnp.where(row == 0, top[7:8], r1)
        gm2 = jnp.where(row == 0, top[6:7], jnp.where(row == 1, top[7:8], r2))
        carry_ref[c] = g[tm - 8:tm]
        tail_ref[0] = g[tm - 8:tm]
    cw = cw_ref[...]
    gc = cb_ref[...] + ((cw[0:1] * gm2 + cw[1:2] * gm1) + cw[2:3] * g)
    act = (jax.nn.silu(gc) * u).astype(BF16)
    acc_ref[...] += _dot(act, wd_ref[...])

    @pl.when(c == pl.num_programs(2) - 1)
    def _():
        out_ref[0] = h_ref[0] + _rms(acc_ref[...], g3_ref[...])


def _ffn(h, g2, wg, wu, cw, cb, wd, g3, prev, tm, tf):
    bk, tk, d = h.shape
    dff = wg.shape[1]
    nc = dff // tf
    short_seq = prev is not None
    tok = pl.BlockSpec((1, tm, d), lambda b, i, c: (b, i, 0))
    const = lambda shape: pl.BlockSpec(shape, lambda b, i, c: (0,) * len(shape))
    in_specs = [tok, const(g2.shape),
                pl.BlockSpec((d, tf), lambda b, i, c: (0, c)),
                pl.BlockSpec((d, tf), lambda b, i, c: (0, c)),
                pl.BlockSpec((CONV_W, tf), lambda b, i, c: (0, c)),
                pl.BlockSpec((1, tf), lambda b, i, c: (0, c)),
                pl.BlockSpec((tf, d), lambda b, i, c: (c, 0)),
                const(g3.shape)]
    args = [h, g2, wg, wu, cw, cb, wd, g3]
    scratch = [pltpu.VMEM((tm, d), BF16), pltpu.VMEM((tm, d), F32)]
    if short_seq:
        in_specs += [pl.BlockSpec((1, tm, tf), lambda b, i, c: (b, i, c))] * 2
        args += list(prev)
        out_specs = [tok, pl.BlockSpec((1, tm, tf), lambda b, i, c: (b, i, c))]
        out_shape = [jax.ShapeDtypeStruct(h.shape, F32), jax.ShapeDtypeStruct((bk, tk, dff), F32)]
    else:
        out_specs = [tok, pl.BlockSpec((1, 8, tf), lambda b, i, c: (b, 0, c))]
        out_shape = [jax.ShapeDtypeStruct(h.shape, F32), jax.ShapeDtypeStruct((bk, 8, dff), F32)]
        scratch.append(pltpu.VMEM((nc, 8, tf), F32))
    return pl.pallas_call(
        functools.partial(_ffn_kernel, short_seq),
        grid=(bk, tk // tm, nc),
        in_specs=in_specs,
        out_specs=out_specs,
        out_shape=out_shape,
        scratch_shapes=scratch,
        compiler_params=_cp(("parallel", "arbitrary", "arbitrary")),
        name="conv_ffn",
    )(*args)


def _flash_cols(i, tq, chains):
    subs = []
    for c in chains:
        c["acc"][...] = jnp.zeros_like(c["acc"])
        for c0 in range(0, c["acc"].shape[1], c["cw"]):
            subs.append((c, slice(c0, c0 + c["cw"])))

    def step(j, stats, masked):
        kv = pl.ds(pl.multiple_of(j * tq, tq), tq)
        new_stats = []
        for (c, cols), (m_prev, l_prev) in zip(subs, stats):
            s = c["qk"](kv, cols)
            if c["scale"] is not None:
                s = s * c["scale"]
            if c["bias"] is not None:
                s = s + c["bias"](kv, cols)
            if masked:
                key = lax.broadcasted_iota(jnp.int32, s.shape, 0)
                qpos = (lax.broadcasted_iota(jnp.int32, s.shape, 1) + cols.start) % tq
                s = jnp.where(key <= qpos, s, NEG)
            m_new = jnp.maximum(m_prev, jnp.max(s, axis=0, keepdims=True))
            alpha = jnp.exp2(m_prev - m_new)
            p = jnp.exp2(s - m_new)
            c["acc"][:, cols] = alpha * c["acc"][:, cols] + _dot(c["vt"](kv), p.astype(BF16))
            new_stats.append((m_new, alpha * l_prev + jnp.sum(p, axis=0, keepdims=True)))
        return tuple(new_stats)

    init = tuple((jnp.full((1, c["cw"]), NEG, F32), jnp.zeros((1, c["cw"]), F32)) for c, _ in subs)
    stats = lax.fori_loop(0, i, lambda j, st: step(j, st, False), init)
    stats = step(i, stats, True)
    outs, k = [], 0
    for c in chains:
        n = c["acc"].shape[1] // c["cw"]
        l = jnp.concatenate([stats[k + t][1] for t in range(n)], axis=1)
        outs.append(c["acc"][...] / l)
        k += n
    return outs


def _transpose_bf16(x):
    return jnp.transpose(x.astype(F32)).astype(BF16)


def _fill_transposed(i, tq, src_fn, dst_ref):
    n_tiles = dst_ref.shape[1] // tq

    @pl.when(i == 0)
    def _():
        def body(c, carry):
            rows = pl.ds(pl.multiple_of(c * tq, tq), tq)
            dst_ref[:, rows] = _transpose_bf16(src_fn(rows))
            return carry

        lax.fori_loop(0, n_tiles, body, 0)


def _chain_scratch(dv, r):
    return [pltpu.VMEM((dv, r), F32)]


def _lambda_value(lam_ref, lam_init):
    lam = lam_ref[...]
    s01 = jnp.sum(lam[0:1] * lam[1:2], axis=-1, keepdims=True)
    s23 = jnp.sum(lam[2:3] * lam[3:4], axis=-1, keepdims=True)
    return jnp.exp(s01) - jnp.exp(s23) + lam_init


def _half_rows(blk, half):
    z = jnp.zeros_like(blk)
    return jnp.concatenate([blk, z] if half == 0 else [z, blk], axis=0)


def _prompt_a_kernel(lam_init, q_ref, k_ref, v_ref, lam_ref, gsub_ref, out_ref, vt_ref, *st):
    i = pl.program_id(1)
    tq = q_ref.shape[1]
    _fill_transposed(i, tq, lambda rows: v_ref[0, rows, :], vt_ref)
    qt = _transpose_bf16(q_ref[0])
    lam_val = _lambda_value(lam_ref, lam_init)
    for heads in (tuple(range(A_HEADS)),):
        chains = []
        for hd in heads:
            slots = [s * A_HEADS + hd for s in range(2)]
            qts = [_half_rows(qt[64 * e:64 * (e + 1)], e % 2) for e in slots]
            ksl = [slice(LANES * (e // 2), LANES * (e // 2 + 1)) for e in slots]

            def qk(kv, cols, qts=qts, ksl=ksl):
                s, lo = divmod(cols.start, tq)
                return _dot(k_ref[0, kv, ksl[s]], qts[s][:, lo:lo + cols.stop - cols.start])

            def vt(kv, hd=hd):
                return vt_ref[A_DV * hd:A_DV * (hd + 1), kv]

            chains.append(dict(qk=qk, vt=vt, scale=None, bias=None, acc=st[hd], cw=FLASH_COLS))
        for hd, ot in zip(heads, _flash_cols(i, tq, chains)):
            d = jnp.transpose(ot[:, 0:tq] - lam_val * ot[:, tq:2 * tq])
            out_ref[0, :, A_DV * hd:A_DV * (hd + 1)] = (_rms(d, gsub_ref[...]) * (1.0 - lam_init)).astype(BF16)


def _prompt_b_kernel(q1_ref, q2_ref, c_ref, out_ref, vt_ref, acc_ref):
    i = pl.program_id(1)
    tq = q1_ref.shape[1]
    _fill_transposed(i, tq, lambda rows: c_ref[0, rows, 0:KV_LORA], vt_ref)
    q1t = _transpose_bf16(q1_ref[0])
    q2t = _transpose_bf16(q2_ref[0])
    q1s = jnp.concatenate([q1t[KV_LORA * hd:KV_LORA * (hd + 1)] for hd in range(B_HEADS)], axis=1)
    q2s = jnp.concatenate([q2t[LANES * hd:LANES * (hd + 1)] for hd in range(B_HEADS)], axis=1)

    def qk(kv, cols):
        return (_dot(c_ref[0, kv, 0:KV_LORA], q1s[:, cols])
                + _dot(c_ref[0, kv, KV_LORA:KV_LORA + LANES], q2s[:, cols]))

    chain = dict(qk=qk, vt=lambda kv: vt_ref[:, kv], scale=(QK_NOPE + QK_ROPE) ** -0.5 * LOG2E, bias=None,
                 acc=acc_ref,
                 cw=B_HEADS * tq)
    (ot,) = _flash_cols(i, tq, [chain])
    for hd in range(B_HEADS):
        out_ref[0, :, KV_LORA * hd:KV_LORA * (hd + 1)] = jnp.transpose(ot[:, tq * hd:tq * (hd + 1)]).astype(BF16)


def _prompt_c_kernel(q_ref, ct_ref, k_ref, v_ref, cn_ref, out_ref, vt_ref, *st):
    i = pl.program_id(1)
    tq = q_ref.shape[1]
    _fill_transposed(i, tq, lambda rows: v_ref[0, rows, :], vt_ref)
    qt = _transpose_bf16(q_ref[0])
    n_pairs = len(st) // 2
    for pp0 in range(0, C_KV_HEADS // 2, n_pairs):
        chains = []
        for pp in range(pp0, pp0 + n_pairs):
            for half in range(2):
                heads = [4 * pp + 2 * half + r for r in range(2)]
                blk = jnp.concatenate(
                    [qt[LANES * (2 * pp + r) + 64 * half:LANES * (2 * pp + r) + 64 * (half + 1)]
                     for r in range(2)], axis=1)
                qs = _half_rows(blk, half)
                cq = [ct_ref[0, hd:hd + 1, :] for hd in heads]

                def bias(kv, cols, heads=heads, cq=cq):
                    return jnp.concatenate([cq[r] - cn_ref[0, kv, hd:hd + 1] for r, hd in enumerate(heads)],
                                           axis=1)

                def qk(kv, cols, qs=qs, pp=pp):
                    return _dot(k_ref[0, kv, LANES * pp:LANES * (pp + 1)], qs[:, cols])

                def vt(kv, g=2 * pp + half):
                    return vt_ref[C_DH * g:C_DH * (g + 1), kv]

                chains.append(dict(qk=qk, vt=vt, scale=None, bias=bias, acc=st[2 * (pp - pp0) + half],
                                   cw=2 * tq))
        outs = _flash_cols(i, tq, chains)
        for pp in range(pp0, pp0 + n_pairs):
            ot = jnp.concatenate(outs[2 * (pp - pp0):2 * (pp - pp0) + 2], axis=0)
            for r in range(2):
                out_ref[0, :, LANES * (2 * pp + r):LANES * (2 * pp + r + 1)] = jnp.transpose(
                    ot[:, tq * r:tq * (r + 1)]).astype(BF16)


def _prompt_attn(kern, args, in_specs, bk, t, out_w, vt_rows, chain_shapes, tq, name):
    scratch = [pltpu.VMEM((vt_rows, t), BF16)]
    for dv, r in chain_shapes:
        scratch += _chain_scratch(dv, r)
    return pl.pallas_call(
        kern,
        grid=(bk, t // tq),
        in_specs=in_specs,
        out_specs=pl.BlockSpec((1, tq, out_w), lambda b, i: (b, i, 0)),
        out_shape=jax.ShapeDtypeStruct((bk, t, out_w), BF16),
        scratch_shapes=scratch,
        compiler_params=_cp(("parallel", "arbitrary")),
        name=name,
    )(*args)


def _q_tile_spec(a, tq):
    return pl.BlockSpec((1, tq, a.shape[-1]), lambda b, i: (b, i, 0))


def _seq_spec(a):
    return pl.BlockSpec((1,) + a.shape[1:], lambda b, i: (b, 0, 0))


def _online_update(s_list, pv_fn, m_ref, l_ref, acc_ref):
    m_prev = m_ref[...]
    m_new = m_prev
    for s in s_list:
        m_new = jnp.maximum(m_new, jnp.max(s, axis=-1, keepdims=True))
    alpha = jnp.exp(m_prev - m_new)
    ps = [jnp.exp(s - m_new) for s in s_list]
    l_new = alpha * l_ref[...]
    for p in ps:
        l_new = l_new + jnp.sum(p, axis=-1, keepdims=True)
    l_ref[...] = l_new
    acc_ref[...] = alpha * acc_ref[...] + pv_fn([p.astype(BF16) for p in ps])
    m_ref[...] = m_new


def _page_groups(n_pp):
    per = -(-n_pp // DECODE_SPLIT)
    return [list(range(g * per, min((g + 1) * per, n_pp))) for g in range(DECODE_SPLIT)]


def _merged_softmax(states):
    ms = [m[...] for m, _, _ in states]
    m_all = functools.reduce(jnp.maximum, ms)
    num, den = None, None
    for m, (_, l_ref, acc_ref) in zip(ms, states):
        w = jnp.exp(m - m_all)
        num = acc_ref[...] * w if num is None else num + acc_ref[...] * w
        den = l_ref[...] * w if den is None else den + l_ref[...] * w
    return num / den


def _init_softmax(m_ref, l_ref, acc_ref):
    m_ref[...] = jnp.full_like(m_ref, NEG)
    l_ref[...] = jnp.zeros_like(l_ref)
    acc_ref[...] = jnp.zeros_like(acc_ref)


def _new_token_mask(s):
    row = lax.broadcasted_iota(jnp.int32, s.shape, 0) % 8
    col = lax.broadcasted_iota(jnp.int32, s.shape, 1)
    return jnp.where(col <= row, s, NEG)


def _decode_ab_kernel(n_pp, lam_init, pt_ref, qa_ref, qb_ref, *refs):
    ak_refs, av_refs, bc_refs = refs[:n_pp], refs[n_pp:2 * n_pp], refs[2 * n_pp:3 * n_pp]
    kn_ref, vn_ref, cn_ref, lam_ref, gsub_ref, oa_ref, ob_ref = refs[3 * n_pp:3 * n_pp + 7]
    st = refs[3 * n_pp + 7:]
    sa = [st[3 * g:3 * g + 3] for g in range(DECODE_SPLIT)]
    sb = [st[3 * (DECODE_SPLIT + g):3 * (DECODE_SPLIT + g) + 3] for g in range(DECODE_SPLIT)]
    j = pl.program_id(1)
    qa = qa_ref[0]
    qb = qb_ref[0]
    scale_b = (QK_NOPE + QK_ROPE) ** -0.5

    def update_a(state, k_pages, v_of, mask):
        s_list = [_dot(qa, k) for k in k_pages]
        if mask:
            s_list = [_new_token_mask(s) for s in s_list]

        def pv(ps):
            outs = []
            for hd in range(A_HEADS):
                o = None
                for p, pr in enumerate(ps):
                    t = _dot(pr[16 * hd:16 * (hd + 1)], v_of(p, hd))
                    o = t if o is None else o + t
                outs.append(o)
            return jnp.concatenate(outs, axis=0)

        _online_update(s_list, pv, *state)

    def update_b(state, pages, mask):
        s_list = [_dot(qb, c) * scale_b for c in pages]
        if mask:
            s_list = [_new_token_mask(s) for s in s_list]

        def pv(ps):
            o = None
            for pr, c in zip(ps, pages):
                t = _dot_nt(pr, c[0:KV_LORA])
                o = t if o is None else o + t
            return o

        _online_update(s_list, pv, *state)

    @pl.when(j == 0)
    def _():
        for state in sa + sb:
            _init_softmax(*state)

    for g, grp in enumerate(_page_groups(n_pp)):
        if not grp:
            continue
        update_a(sa[g], [ak_refs[p][0, 0].astype(BF16) for p in grp],
                 lambda p, hd, grp=grp: av_refs[grp[p]][0, 0, pl.ds(hd, PAGE, stride=A_HEADS), :].astype(BF16),
                 False)
        update_b(sb[g], [bc_refs[p][0, 0].astype(BF16) for p in grp], False)

    @pl.when(j == pl.num_programs(1) - 1)
    def _():
        update_a(sa[0], [kn_ref[0].astype(BF16)],
                 lambda p, hd: vn_ref[0, pl.ds(hd, PAGE, stride=A_HEADS), :].astype(BF16), True)
        update_b(sb[0], [cn_ref[0].astype(BF16)], True)
        o = _merged_softmax(sa)
        lam_val = _lambda_value(lam_ref, lam_init)
        for hd in range(A_HEADS):
            d = o[16 * hd:16 * hd + 8] - lam_val * o[16 * hd + 8:16 * hd + 16]
            oa_ref[0, :, A_DV * hd:A_DV * (hd + 1)] = (
                _rms(d, gsub_ref[...]) * (1.0 - lam_init)).astype(BF16)
        o = _merged_softmax(sb)
        for hd in range(B_HEADS):
            ob_ref[0, :, KV_LORA * hd:KV_LORA * (hd + 1)] = o[8 * hd:8 * (hd + 1)].astype(BF16)


def _expand_heads(x):
    return jnp.broadcast_to(x[:, None, :], (C_HEADS, 8, x.shape[-1])).reshape(C_HEADS * 8, x.shape[-1])


def _decode_c_kernel(n_pp, pt_ref, q_ref, *refs):
    k_refs, v_refs, lf_refs = refs[:n_pp], refs[n_pp:2 * n_pp], refs[2 * n_pp:3 * n_pp]
    kn_ref, vn_ref, lfn_ref, tri_ge_ref, tri_le_ref, out_ref = refs[3 * n_pp:3 * n_pp + 6]
    st = refs[3 * n_pp + 6:]
    states = [st[3 * g:3 * g + 3] for g in range(DECODE_SPLIT)]
    base_ref, bq_ref = st[3 * DECODE_SPLIT:]
    j = pl.program_id(1)
    last = pl.num_programs(1) - 1
    q = q_ref[0]

    def scores(k_page):
        return _dot(q, k_page)

    def pv_of(pages):
        def pv(ps):
            o = None
            for pr, v in zip(ps, pages):
                t = _dot_nt(pr, v)
                o = t if o is None else o + t
            return o
        return pv

    @pl.when(j == 0)
    def _():
        for state in states:
            _init_softmax(*state)
        base_ref[...] = jnp.zeros_like(base_ref)
        cnl = _tri_dot(lfn_ref[0], tri_le_ref[...])
        rep = _expand_heads(cnl)
        row = lax.broadcasted_iota(jnp.int32, rep.shape, 0) % 8
        col = lax.broadcasted_iota(jnp.int32, rep.shape, 1)
        bq = jnp.sum(jnp.where(col == row, rep, 0.0), axis=-1, keepdims=True)
        bq_ref[...] = bq
        s_new = _new_token_mask(scores(kn_ref[0].astype(BF16)) + (bq - rep))
        _online_update([s_new], pv_of([vn_ref[0].astype(BF16)]), *states[0])

    lf = jnp.concatenate([lf_refs[p][0, 0] for p in range(n_pp)], axis=0)
    incl = _tri_dot(lf, tri_ge_ref[...])
    excl = incl - lf
    bq = bq_ref[...]
    base = base_ref[...]
    s_list = []
    for p in range(n_pp):
        sl = slice(C_HEADS * p, C_HEADS * (p + 1))
        bias = _expand_heads(base + excl[sl])
        s_list.append(scores(k_refs[p][0, 0].astype(BF16)) + (bq + bias))
        base = base + incl[sl, 0:1]
    base_ref[...] = base
    for g, grp in enumerate(_page_groups(n_pp)):
        if grp:
            _online_update([s_list[p] for p in grp], pv_of([v_refs[p][0, 0].astype(BF16) for p in grp]),
                           *states[g])

    @pl.when(j == last)
    def _():
        o = _merged_softmax(states)
        lane = lax.broadcasted_iota(jnp.int32, (8, LANES), 1)
        for pp in range(C_KV_HEADS // 2):
            csl = slice(LANES * pp, LANES * (pp + 1))
            for r in range(2):
                h0, h1 = 4 * pp + r, 4 * pp + 2 + r
                chunk = jnp.where(lane < 64, o[8 * h0:8 * (h0 + 1), csl], o[8 * h1:8 * (h1 + 1), csl])
                out_ref[0, :, LANES * (2 * pp + r):LANES * (2 * pp + r + 1)] = chunk.astype(BF16)


def _softmax_scratch(rows, dv):
    return [pltpu.VMEM((rows, 1), F32), pltpu.VMEM((rows, 1), F32), pltpu.VMEM((rows, dv), F32)]


def _decode_attn(kern, page_table, qs, caches, news, consts, out_ws, scratch, n_pp, reverse, name):
    nb, n_pages = page_table.shape
    n_steps = n_pages // n_pp

    def page_spec(c, p):
        def imap(b, j, pt):
            idx = j * n_pp + p
            if reverse:
                idx = n_pages - 1 - idx
            return (0, pt[b, idx], 0, 0)
        return pl.BlockSpec((1, 1) + c.shape[2:], imap)

    in_specs, args = [], []
    for q in qs:
        in_specs.append(pl.BlockSpec((1,) + q.shape[1:], lambda b, j, pt: (b, 0, 0)))
        args.append(q)
    for c in caches:
        for p in range(n_pp):
            in_specs.append(page_spec(c, p))
            args.append(c)
    for a in news:
        in_specs.append(pl.BlockSpec((1,) + a.shape[1:], lambda b, j, pt: (b, 0, 0)))
        args.append(a)
    for a in consts:
        in_specs.append(pl.BlockSpec(a.shape, lambda b, j, pt, nd=a.ndim: (0,) * nd))
        args.append(a)
    grid_spec = pltpu.PrefetchScalarGridSpec(
        num_scalar_prefetch=1,
        grid=(nb, n_steps),
        in_specs=in_specs,
        out_specs=[pl.BlockSpec((1, 8, w), lambda b, j, pt: (b, 0, 0)) for w in out_ws],
        scratch_shapes=scratch,
    )
    return pl.pallas_call(
        kern,
        grid_spec=grid_spec,
        out_shape=[jax.ShapeDtypeStruct((nb, 8, w), BF16) for w in out_ws],
        compiler_params=_cp(("parallel", "arbitrary")),
        name=name,
    )(page_table, *args)


_HEAD_OF_SLOT = [4 * (c // 2) + 2 * half + (c % 2) for c in range(C_HEADS // 2) for half in range(2)]
_SLOT_OF_HEAD = [_HEAD_OF_SLOT.index(h) for h in range(C_HEADS)]


def _lambda_init(layer):
    return 0.8 - 0.6 * math.exp(-0.3 * layer)


def _pad_axis(x, axis, size):
    pad = [(0, 0)] * x.ndim
    pad[axis] = (0, size - x.shape[axis])
    return jnp.pad(x, pad)


def _tok_tile(n, pref):
    tm = min(n, pref)
    assert n % tm == 0 and tm % 8 == 0
    return tm


def _feature_major_page(x):
    return _pad_axis(jnp.swapaxes(x, 1, 2), 2, PAGE)


def kernel(x_prompt, x_sample, cache_a_k, cache_a_v, cache_b_ckv, cache_c_k, cache_c_v, cache_c_logf, state_conv, page_table, norm_gains, w_in_even, w_q_up, w_kv_uk, w_kv_uv, g_q_lat, g_kv_lat, diff_lambda, g_diff_subln, w_out_even, w_in_odd, b_forget, w_out_odd, ffn_w_gate, ffn_w_up, ffn_conv_w, ffn_conv_b, ffn_w_down):
    depth = norm_gains.shape[0]
    bp, tp, d = x_prompt.shape
    bs, ts, _ = x_sample.shape
    n_pages = page_table.shape[1]
    n_pool = cache_a_k.shape[1]
    dff = ffn_w_gate.shape[2]
    assert ts == 8 and cache_a_k.shape[2] == PAGE
    ns = bs * ts
    tm_p, tm_s = _tok_tile(tp, 512), _tok_tile(ns, 512)
    tq = _tok_tile(tp, 512)
    tf = dff // 2 if dff % (2 * LANES) == 0 else dff
    n_pp = 32 if n_pages % 32 == 0 else n_pages
    assert dff % tf == 0

    pos_p = jnp.arange(tp, dtype=jnp.int32)
    pos_s = n_pages * PAGE + jnp.arange(ts, dtype=jnp.int32)
    tabs_p = (_rope_tables(pos_p, A_ROT), _rope_tables(pos_p, QK_ROPE))
    tabs_s = tuple(tuple(jnp.tile(t, (tm_s // ts, 1)) for t in tabs)
                   for tabs in (_rope_tables(pos_s, A_ROT), _rope_tables(pos_s, QK_ROPE)))

    akT = jnp.transpose(cache_a_k, (0, 1, 3, 4, 5, 2)).reshape(-1, n_pool, 512, PAGE)
    av4 = cache_a_v.reshape(-1, n_pool, PAGE * A_HEADS, A_DV)
    bcT = jnp.swapaxes(cache_b_ckv, 2, 3)
    ckT = jnp.transpose(cache_c_k, (0, 1, 3, 4, 2)).reshape(-1, n_pool, 512, PAGE)
    cvT = jnp.transpose(cache_c_v, (0, 1, 3, 4, 2)).reshape(-1, n_pool, 512, PAGE)
    cfT = jnp.swapaxes(cache_c_logf, 2, 3)

    tri_ge = jnp.tril(jnp.ones((PAGE, PAGE), F32)).astype(BF16)
    tri_le = jnp.triu(jnp.ones((PAGE, PAGE), F32)).astype(BF16)

    hp = x_prompt
    hs = x_sample.reshape(1, ns, d)
    outs_p = {k: [] for k in ("ak", "av", "ckv", "ck", "cv", "cf", "conv")}
    outs_s = {k: [] for k in outs_p}

    for li in range(depth):
        gn = norm_gains[li]
        g0, g1, g2, g3 = (gn[k].reshape(1, d) for k in range(4))
        if li % 2 == 0:
            e = li // 2
            lam_init = _lambda_init(li)
            win = _pad_axis(w_in_even[e], 1, 2304).astype(BF16)
            wq_n = w_q_up[e][:, :, :QK_NOPE].reshape(Q_LORA, B_HEADS * QK_NOPE)
            wq_r = _pad_axis(w_q_up[e][:, :, QK_NOPE:], 2, LANES).reshape(Q_LORA, B_HEADS * LANES)
            wq = jnp.concatenate([wq_n, wq_r], axis=1).astype(BF16)
            wuk = jnp.transpose(w_kv_uk[e], (1, 2, 0)).astype(BF16)
            wuv = jnp.transpose(w_kv_uv[e], (1, 0, 2)).astype(BF16)
            wout = w_out_even[e].astype(BF16)
            gq, gkv = g_q_lat[e].reshape(1, -1), g_kv_lat[e].reshape(1, -1)
            lam, gsub = diff_lambda[e], g_diff_subln[e].reshape(1, -1)

            def even_in(h, tabs, tm, logit_unit):
                return _even_in(h, g0, win, gq, gkv, wq, wuk, tabs[0], tabs[1], tm, logit_unit)

            qa, ka, kab, va, vab, qlat, qrp, ckv, ckvb = even_in(hp, tabs_p, tm_p, LOG2E)
            oa = _prompt_attn(functools.partial(_prompt_a_kernel, lam_init), [qa, kab, vab, lam, gsub],
                              [_q_tile_spec(qa, tq), _seq_spec(kab), _seq_spec(vab), _full_spec(lam.shape),
                               _full_spec(gsub.shape)],
                              bp, tp, 512, 512, [(A_DV, 2 * tq)] * A_HEADS, tq, "prompt_attn_a")
            olat = _prompt_attn(_prompt_b_kernel, [qlat, qrp, ckvb],
                                [_q_tile_spec(qlat, tq), _q_tile_spec(qrp, tq), _seq_spec(ckvb)],
                                bp, tp, 1024, KV_LORA, [(KV_LORA, B_HEADS * tq)], tq, "prompt_attn_b")
            mix_p = (oa, olat)
            outs_p["ak"].append(ka.reshape(bp, tp, 2, A_HEADS, A_DH))
            outs_p["av"].append(va.reshape(bp, tp, A_HEADS, A_DV))
            outs_p["ckv"].append(ckv)

            qa, ka, kab, va, vab, qlat, qrp, ckv, ckvb = even_in(hs, tabs_s, tm_s, 1.0)
            sel = (jnp.arange(8)[:, None] == jnp.arange(8).reshape(2, 4).T.reshape(8)[None, :])
            qx = jnp.transpose(qa.reshape(bs, ts, 2, A_HEADS, A_DH), (0, 3, 2, 1, 4)).reshape(bs, 8, ts, A_DH)
            q_bd = jnp.where(sel.T[None, :, None, :, None], qx[:, :, :, None, :], jnp.zeros((), BF16))
            q_bd = q_bd.reshape(bs, 8 * ts, 512)
            ckv3 = ckv.reshape(bs, ts, 320)
            kn = _feature_major_page(kab.reshape(bs, ts, 512))
            vn = _pad_axis(va.reshape(bs, ts * A_HEADS, A_DV), 1, PAGE * A_HEADS)
            cn_page = _feature_major_page(ckvb.reshape(bs, ts, KV_LORA + LANES)[:, :, :KV_LORA + QK_ROPE])
            q_abs = jnp.concatenate([qlat.reshape(bs, ts, B_HEADS, KV_LORA),
                                     qrp.reshape(bs, ts, B_HEADS, LANES)[..., :QK_ROPE]], axis=-1)
            q_abs = jnp.swapaxes(q_abs, 1, 2).reshape(bs, B_HEADS * ts, KV_LORA + QK_ROPE)
            oa, olat = _decode_attn(functools.partial(_decode_ab_kernel, n_pp, lam_init), page_table, [q_bd, q_abs],
                                    [akT[e:e + 1], av4[e:e + 1], bcT[e:e + 1]],
                                    [kn, vn, cn_page], [lam, gsub], [512, 1024],
                                    _softmax_scratch(64, A_DV) * DECODE_SPLIT
                                    + _softmax_scratch(32, KV_LORA) * DECODE_SPLIT, n_pp, False,
                                    "decode_attn_ab")
            mix_s = (oa.reshape(1, ns, 512), olat.reshape(1, ns, 1024))
            outs_s["ak"].append(ka.reshape(bs, ts, 2, A_HEADS, A_DH))
            outs_s["av"].append(va.reshape(bs, ts, A_HEADS, A_DV))
            outs_s["ckv"].append(ckv3)

            hp = _mix_out(_even_out_kernel, hp, mix_p, [wuv, wout, g1], tm_p, "even_out_proj")
            hs = _mix_out(_even_out_kernel, hs, mix_s, [wuv, wout, g1], tm_s, "even_out_proj")
        else:
            o = li // 2
            w = w_in_odd[o]
            nq, nk = C_HEADS * C_DH, C_KV_HEADS * C_DH
            wq_perm = w[:, :nq].reshape(d, C_HEADS, C_DH)[:, jnp.array(_HEAD_OF_SLOT), :].reshape(d, nq)
            win = jnp.concatenate([wq_perm, w[:, nq:nq + 2 * nk]], axis=1).astype(BF16)
            wft = jnp.transpose(w[:, nq + 2 * nk:]).astype(BF16)
            bf = b_forget[o].reshape(C_HEADS, 1)
            wout = w_out_odd[o].reshape(C_HEADS, C_DH, d)[jnp.array(_HEAD_OF_SLOT)].reshape(nq, d).astype(BF16)

            q, k, kb, v, vb, lft, ct, cn = _odd_in(hp, g0, win, wft, bf, tm_p, LOG2E)
            op = _prompt_attn(_prompt_c_kernel, [q, ct, kb, vb, cn],
                              [_q_tile_spec(q, tq), pl.BlockSpec((1, C_HEADS, tq), lambda b, i: (b, 0, i)),
                               _seq_spec(kb), _seq_spec(vb), _seq_spec(cn)],
                              bp, tp, 1024, 512, [(C_DH, 2 * tq)] * 8, tq, "prompt_attn_c")
            outs_p["ck"].append(k.reshape(bp, tp, C_KV_HEADS, C_DH))
            outs_p["cv"].append(v.reshape(bp, tp, C_KV_HEADS, C_DH))
            outs_p["cf"].append(jnp.swapaxes(lft, 1, 2))

            q, k, kb, v, vb, lft, ct, cn = _odd_in(hs, g0, win, wft, bf, tm_s, 1.0)
            qh = q.reshape(bs, ts, C_HEADS, C_DH)[:, :, jnp.array(_SLOT_OF_HEAD), :]
            qh = jnp.swapaxes(qh, 1, 2)
            selc = (jnp.arange(C_HEADS)[:, None] // 2 == jnp.arange(C_KV_HEADS)[None, :])
            q_bd = jnp.where(selc[None, :, None, :, None], qh[:, :, :, None, :], jnp.zeros((), BF16))
            q_bd = q_bd.reshape(bs, C_HEADS * ts, 512)
            k3, v3 = k.reshape(bs, ts, 512), v.reshape(bs, ts, 512)
            lf3 = jnp.swapaxes(lft.reshape(C_HEADS, bs, ts), 0, 1)
            (os_,) = _decode_attn(functools.partial(_decode_c_kernel, n_pp), page_table, [q_bd],
                                  [ckT[o:o + 1], cvT[o:o + 1], cfT[o:o + 1]],
                                  [_feature_major_page(kb.reshape(bs, ts, 512)),
                                   _feature_major_page(vb.reshape(bs, ts, 512)), _pad_axis(lf3, 2, PAGE)],
                                  [tri_ge, tri_le], [1024],
                                  _softmax_scratch(C_HEADS * 8, 512) * DECODE_SPLIT
                                  + [pltpu.VMEM((C_HEADS, 1), F32), pltpu.VMEM((C_HEADS * 8, 1), F32)],
                                  n_pp, True, "decode_attn_c")
            outs_s["ck"].append(k3.reshape(bs, ts, C_KV_HEADS, C_DH))
            outs_s["cv"].append(v3.reshape(bs, ts, C_KV_HEADS, C_DH))
            outs_s["cf"].append(jnp.swapaxes(lf3, 1, 2))

            hp = _mix_out(_odd_out_kernel, hp, (op,), [wout, g1], tm_p, "odd_out_proj")
            hs = _mix_out(_odd_out_kernel, hs, (os_.reshape(1, ns, nq),), [wout, g1], tm_s, "odd_out_proj")

        wg, wu, wd = ffn_w_gate[li].astype(BF16), ffn_w_up[li].astype(BF16), ffn_w_down[li].astype(BF16)
        cw, cb = ffn_conv_w[li], ffn_conv_b[li].reshape(1, dff)
        hp, tail = _ffn(hp, g2, wg, wu, cw, cb, wd, g3, None, tm_p, tf)
        outs_p["conv"].append(tail[:, 8 - (CONV_W - 1):])
        buf = state_conv[li]
        prev2 = _pad_axis(buf, 1, ts).reshape(1, ns, dff)
        prev1 = _pad_axis(buf[:, 1:], 1, ts).reshape(1, ns, dff)
        hs, gate = _ffn(hs, g2, wg, wu, cw, cb, wd, g3, (prev2, prev1), tm_s, tf)
        outs_s["conv"].append(gate.reshape(bs, ts, dff)[:, ts - (CONV_W - 1):])

    st = lambda xs: jnp.stack(xs)
    res = [hp, hs.reshape(bs, ts, d)]
    for key in ("ak", "av", "ckv", "ck", "cv", "cf", "conv"):
        res += [st(outs_p[key]), st(outs_s[key])]
    return tuple(res)
```
